```python
import jax, jax.numpy as jnp
from jax import lax
import numpy as np

D_MODEL = 2048
BATCH = 2
SEQ = 8192
DEPTH = 1
DEC_BATCH = 32
DEC_SEQ = 4
PAST_LEN = 16384
PAGE_SIZE = 128

HEAD_DIM = 128
N_HEADS = 8
GROUPS = ((128, 1), (512, 4), (2048, 16))
N_GROUPS = 3
N_OFFSETS = 129
MAX_OFFSET = N_OFFSETS - 1
BAND = 128
D_ATT = N_HEADS * HEAD_DIM
D_CONV = D_MODEL
CONV_WIDTH = 3
ROT_DIM = HEAD_DIM // 4
ROPE_THETA = 500000.0
EPS = 1e-6
SCALE = HEAD_DIM ** -0.5
NEG_INF = -1e30

QKV_COLS = N_GROUPS * 3 * D_ATT
OFF_AGATE = QKV_COLS
OFF_CONV = OFF_AGATE + D_ATT
OFF_CGATE = OFF_CONV + 3 * D_CONV
OFF_MERGE = OFF_CGATE + D_CONV
D_IN_TOTAL = OFF_MERGE + 2 * D_MODEL

kernel_name = "dilated_swa_shortconv_gated_hybrid_step"


def _rmsnorm(x, g):
    x32 = x.astype(jnp.float32)
    y = x32 * lax.rsqrt(jnp.mean(x32 * x32, axis=-1, keepdims=True) + EPS)
    return (y * g.astype(jnp.float32)).astype(x.dtype)


def _rope(x, pos):
    half = ROT_DIM // 2
    inv = ROPE_THETA ** (-jnp.arange(half, dtype=jnp.float32) * (2.0 / ROT_DIM))
    ang = pos[:, None] * inv[None, :]
    cos = jnp.cos(ang)[None, :, None, :]
    sin = jnp.sin(ang)[None, :, None, :]
    x32 = x.astype(jnp.float32)
    x1, x2, rest = x32[..., :half], x32[..., half:ROT_DIM], x32[..., ROT_DIM:]
    return jnp.concatenate([x1 * cos - x2 * sin, x2 * cos + x1 * sin, rest], axis=-1).astype(x.dtype)


def _softmax_stats(s):
    m = jnp.max(s, axis=-1, keepdims=True)
    p = jnp.exp(s - m)
    l = jnp.sum(p, axis=-1, keepdims=True)
    return p / l, (m + jnp.log(l))[..., 0]


def _mixer_inputs(x, pos, norm_w, w_in, q_norm_w, k_norm_w):
    b, s, _ = x.shape
    z = _rmsnorm(x, norm_w) @ w_in
    qkv = z[..., :QKV_COLS].reshape(b, s, N_GROUPS, 3, N_HEADS, HEAD_DIM)
    q = _rmsnorm(qkv[:, :, :, 0], q_norm_w[:, None, :]).reshape(b, s, N_GROUPS * N_HEADS, HEAD_DIM)
    k = _rmsnorm(qkv[:, :, :, 1], k_norm_w[:, None, :]).reshape(b, s, N_GROUPS * N_HEADS, HEAD_DIM)
    q, k = _rope(q, pos), _rope(k, pos)
    v = qkv[:, :, :, 2]
    qs = [q[:, :, g * N_HEADS:(g + 1) * N_HEADS] for g in range(N_GROUPS)]
    ks = [k[:, :, g * N_HEADS:(g + 1) * N_HEADS] for g in range(N_GROUPS)]
    vs = [v[:, :, g] for g in range(N_GROUPS)]
    a_gate = z[..., OFF_AGATE:OFF_CONV]
    h_c = z[..., OFF_CONV:OFF_CONV + D_CONV]
    b_c = z[..., OFF_CONV + D_CONV:OFF_CONV + 2 * D_CONV]
    c_c = z[..., OFF_CONV + 2 * D_CONV:OFF_CGATE]
    c_gate = z[..., OFF_CGATE:OFF_MERGE]
    merge_logits = z[..., OFF_MERGE:]
    return qs, ks, vs, a_gate, h_c, b_c, c_c, c_gate, merge_logits


def _dilated_band_attention(q, k, v, dilation):
    b, s, h, dh = q.shape
    L = s // dilation
    nb = -(-L // BAND)
    lp = nb * BAND
    n = b * dilation

    def to_sub(x):
        x = x.reshape(b, L, dilation, h, dh).transpose(0, 2, 1, 3, 4).reshape(n, L, h, dh)
        return jnp.pad(x, ((0, 0), (0, lp - L), (0, 0), (0, 0)))

    def band_keys(x):
        xp = jnp.pad(to_sub(x), ((0, 0), (BAND, 0), (0, 0), (0, 0)))
        prev = xp[:, :lp].reshape(n, nb, BAND, h, dh)
        cur = xp[:, BAND:].reshape(n, nb, BAND, h, dh)
        return jnp.concatenate([prev, cur], axis=2)

    qb = to_sub(q).reshape(n, nb, BAND, h, dh)
    kb, vb = band_keys(k), band_keys(v)
    sc = jnp.einsum('nbqhd,nbkhd->nbhqk', qb, kb).astype(jnp.float32) * SCALE
    qi = jnp.arange(BAND)[:, None]
    kj = jnp.arange(2 * BAND)[None, :]
    dist = qi + BAND - kj
    key_idx = jnp.arange(nb)[:, None, None] * BAND - BAND + kj[None]
    mask = ((dist >= 0) & (dist <= MAX_OFFSET))[None] & (key_idx >= 0)
    sc = jnp.where(mask[None, :, None], sc, NEG_INF)
    p, lse = _softmax_stats(sc)
    o = jnp.einsum('nbhqk,nbkhd->nbqhd', p.astype(v.dtype), vb)
    o = o.reshape(n, lp, h, dh)[:, :L].reshape(b, dilation, L, h, dh).transpose(0, 2, 1, 3, 4).reshape(b, s, h, dh)
    lse = lse.transpose(0, 1, 3, 2).reshape(n, lp, h)[:, :L].reshape(b, dilation, L, h).transpose(0, 2, 1, 3).reshape(b, s, h)
    return o, lse


def _dilated_gather_attention(q, k_ext, v_ext, dilation):
    t = q.shape[1]
    lb = k_ext.shape[1] - t
    idx = lb + jnp.arange(t)[:, None] - jnp.arange(N_OFFSETS)[None, :] * dilation
    valid = idx >= 0
    idx = jnp.maximum(idx, 0)
    kg = k_ext[:, idx]
    vg = v_ext[:, idx]
    sc = jnp.einsum('bthd,btjhd->bthj', q, kg).astype(jnp.float32) * SCALE
    sc = jnp.where(valid[None, :, None, :], sc, NEG_INF)
    p, lse = _softmax_stats(sc)
    o = jnp.einsum('bthj,btjhd->bthd', p.astype(v_ext.dtype), vg)
    return o, lse


def _combine_groups(outs, lses):
    w = jax.nn.softmax(jnp.stack(lses, axis=0), axis=0)
    return jnp.einsum('gbsh,gbshd->bshd', w.astype(outs[0].dtype), jnp.stack(outs, axis=0))


def _short_conv(u_ext, w):
    L = u_ext.shape[1] - (CONV_WIDTH - 1)
    return sum(w[i] * u_ext[:, i:i + L] for i in range(CONV_WIDTH))


def _merge(x, o_att, a_gate, conv_y, c_gate, merge_logits, w_att_proj, w_conv_proj, w_out):
    b, s, _ = x.shape
    a = (o_att.reshape(b, s, D_ATT) * jax.nn.silu(a_gate)) @ w_att_proj
    c = (conv_y * jax.nn.silu(c_gate)) @ w_conv_proj
    m = jax.nn.sigmoid(merge_logits[..., :D_MODEL]) * a + jax.nn.sigmoid(merge_logits[..., D_MODEL:]) * c
    return x + m @ w_out


def setup_inputs(seed: int = 0) -> dict:
    key = jax.random.key(seed)
    ks = jax.random.split(key, 16)
    f32 = jnp.float32

    def nrm(k, shape, scale):
        return jax.random.normal(k, shape, f32) * scale

    def kv_cache(k, window):
        return nrm(k, (DEPTH, DEC_BATCH, min(window, PAST_LEN), 2, N_HEADS, HEAD_DIM), 1.0)

    return {
        "x_prompt": nrm(ks[0], (BATCH, SEQ, D_MODEL), 1.0),
        "x_sample": nrm(ks[1], (DEC_BATCH, DEC_SEQ, D_MODEL), 1.0),
        "cache_kv_w128": kv_cache(ks[2], GROUPS[0][0]),
        "cache_kv_w512": kv_cache(ks[3], GROUPS[1][0]),
        "cache_kv_w2048": kv_cache(ks[4], GROUPS[2][0]),
        "state_conv": nrm(ks[5], (DEPTH, DEC_BATCH, CONV_WIDTH - 1, D_CONV), 1.0),
        "norm_w": 1.0 + nrm(ks[6], (DEPTH, D_MODEL), 0.1),
        "w_in": nrm(ks[7], (DEPTH, D_MODEL, D_IN_TOTAL), D_MODEL ** -0.5),
        "q_norm_w": 1.0 + nrm(ks[8], (DEPTH, N_GROUPS, HEAD_DIM), 0.1),
        "k_norm_w": 1.0 + nrm(ks[9], (DEPTH, N_GROUPS, HEAD_DIM), 0.1),
        "conv_w": nrm(ks[10], (DEPTH, CONV_WIDTH, D_CONV), CONV_WIDTH ** -0.5),
        "w_att_proj": nrm(ks[11], (DEPTH, D_ATT, D_MODEL), D_ATT ** -0.5),
        "w_conv_proj": nrm(ks[12], (DEPTH, D_CONV, D_MODEL), D_CONV ** -0.5),
        "w_out": nrm(ks[13], (DEPTH, D_MODEL, D_MODEL), D_MODEL ** -0.5),
    }


def reference(x_prompt, x_sample, cache_kv_w128, cache_kv_w512, cache_kv_w2048, state_conv,
              norm_w, w_in, q_norm_w, k_norm_w, conv_w, w_att_proj, w_conv_proj, w_out):
    caches = (cache_kv_w128, cache_kv_w512, cache_kv_w2048)
    s_p = x_prompt.shape[1]
    t_s = x_sample.shape[1]
    pos_p = jnp.arange(s_p, dtype=jnp.float32)
    pos_s = PAST_LEN + jnp.arange(t_s, dtype=jnp.float32)

    yp, ys = x_prompt, x_sample
    kv_p = [[] for _ in GROUPS]
    kv_s = [[] for _ in GROUPS]
    conv_p, conv_s = [], []
    for l in range(DEPTH):
        lw = (norm_w[l], w_in[l], q_norm_w[l], k_norm_w[l])
        pw = (w_att_proj[l], w_conv_proj[l], w_out[l])

        qs, ks, vs, a_gate, h_c, b_c, c_c, c_gate, mlog = _mixer_inputs(yp, pos_p, *lw)
        outs, lses = [], []
        for g, (window, dil) in enumerate(GROUPS):
            o, lse = _dilated_band_attention(qs[g], ks[g], vs[g], dil)
            outs.append(o)
            lses.append(lse)
            keep = min(window, s_p)
            kv_p[g].append(jnp.stack([ks[g][:, -keep:], vs[g][:, -keep:]], axis=2))
        o_att = _combine_groups(outs, lses)
        u_ext = jnp.pad(c_c * h_c, ((0, 0), (CONV_WIDTH - 1, 0), (0, 0)))
        conv_y = b_c * _short_conv(u_ext, conv_w[l])
        conv_p.append(u_ext[:, -(CONV_WIDTH - 1):])
        yp_next = _merge(yp, o_att, a_gate, conv_y, c_gate, mlog, *pw)

        qs, ks, vs, a_gate, h_c, b_c, c_c, c_gate, mlog = _mixer_inputs(ys, pos_s, *lw)
        outs, lses = [], []
        for g, (window, dil) in enumerate(GROUPS):
            buf = caches[g][l]
            k_ext = jnp.concatenate([buf[:, :, 0], ks[g]], axis=1)
            v_ext = jnp.concatenate([buf[:, :, 1], vs[g]], axis=1)
            o, lse = _dilated_gather_attention(qs[g], k_ext, v_ext, dil)
            outs.append(o)
            lses.append(lse)
            keep = min(window, buf.shape[1] + t_s)
            kv_s[g].append(jnp.stack([k_ext[:, -keep:], v_ext[:, -keep:]], axis=2))
        o_att = _combine_groups(outs, lses)
        u_ext = jnp.concatenate([state_conv[l], c_c * h_c], axis=1)
        conv_y = b_c * _short_conv(u_ext, conv_w[l])
        conv_s.append(u_ext[:, -(CONV_WIDTH - 1):])
        ys_next = _merge(ys, o_att, a_gate, conv_y, c_gate, mlog, *pw)

        yp, ys = yp_next, ys_next

    new_kv_w128_prompt = jnp.stack(kv_p[0], axis=0)
    new_kv_w512_prompt = jnp.stack(kv_p[1], axis=0)
    new_kv_w2048_prompt = jnp.stack(kv_p[2], axis=0)
    new_conv_prompt = jnp.stack(conv_p, axis=0)
    new_kv_w128_sample = jnp.stack(kv_s[0], axis=0)
    new_kv_w512_sample = jnp.stack(kv_s[1], axis=0)
    new_kv_w2048_sample = jnp.stack(kv_s[2], axis=0)
    new_conv_sample = jnp.stack(conv_s, axis=0)
    return (yp, ys, new_kv_w128_prompt, new_kv_w512_prompt, new_kv_w2048_prompt, new_conv_prompt,
            new_kv_w128_sample, new_kv_w512_sample, new_kv_w2048_sample, new_conv_sample)
```

```python
import functools

import jax
import jax.numpy as jnp
from jax import lax
from jax.experimental import pallas as pl
from jax.experimental.pallas import tpu as pltpu

F32 = jnp.float32
BF16 = jnp.bfloat16

HEAD_DIM = 128
N_HEADS = 8
GROUPS = ((128, 1), (512, 4), (2048, 16))
N_GROUPS = len(GROUPS)
BAND = 128
D_ATT = N_HEADS * HEAD_DIM
ROT_DIM = HEAD_DIM // 4
ROT_HALF = ROT_DIM // 2
ROPE_THETA = 500000.0
EPS = 1e-6
SCALE = HEAD_DIM ** -0.5
NEG_INF = -1e30
PAST_LEN = 16384
CONV_WIDTH = 3

V7X_VMEM_BYTES = 64 * 1024 * 1024
SUBLANES = 8

SUPER = BAND * max(d for _, d in GROUPS)
CHUNK = 512


def _vmem_limit(estimate_bytes):
    return int(min(V7X_VMEM_BYTES - 8 * 1024 * 1024, max(32 * 1024 * 1024, estimate_bytes)))


def _proj_body(x_ref, g_ref, w_ref, o_ref, xn_ref, *, chunk):
    tm = x_ref.shape[0]

    @pl.when(pl.program_id(1) == 0)
    def _normalise():
        def step(c, carry):
            r0 = pl.multiple_of(c * chunk, chunk)
            x = x_ref[pl.ds(r0, chunk), :]
            ms = jnp.mean(x * x, axis=-1, keepdims=True)
            xn_ref[pl.ds(r0, chunk), :] = ((x * lax.rsqrt(ms + EPS)) * g_ref[...]).astype(BF16)
            return carry
        lax.fori_loop(0, tm // chunk, step, 0)

    o_ref[...] = jnp.dot(xn_ref[...], w_ref[...], preferred_element_type=F32)


def _proj(x, g, w_bf16, *, tm, tn):
    m, d = x.shape
    n = w_bf16.shape[1]
    assert m % tm == 0 and n % tn == 0
    chunk = min(tm, 128)
    est = 2 * tm * d * 4 + tm * d * 2 + 2 * d * tn * 2 + 2 * tm * tn * 4 + (4 << 20)
    return pl.pallas_call(
        functools.partial(_proj_body, chunk=chunk),
        grid=(m // tm, n // tn),
        in_specs=[
            pl.BlockSpec((tm, d), lambda i, j: (i, 0)),
            pl.BlockSpec((1, d), lambda i, j: (0, 0)),
            pl.BlockSpec((d, tn), lambda i, j: (0, j)),
        ],
        out_specs=pl.BlockSpec((tm, tn), lambda i, j: (i, j)),
        out_shape=jax.ShapeDtypeStruct((m, n), F32),
        scratch_shapes=[pltpu.VMEM((tm, d), BF16)],
        compiler_params=pltpu.CompilerParams(
            dimension_semantics=("arbitrary", "arbitrary"),
            vmem_limit_bytes=_vmem_limit(est)),
        name="proj",
    )(x, g, w_bf16)


def _rope_tables(pos):
    n = pos.shape[0]
    inv = ROPE_THETA ** (-jnp.arange(ROT_HALF, dtype=F32) * (2.0 / ROT_DIM))
    ang = pos[:, None] * inv[None, :]
    cos, sin = jnp.cos(ang), jnp.sin(ang)
    c = jnp.concatenate([cos, cos, jnp.ones((n, HEAD_DIM - ROT_DIM), F32)], axis=1)
    sa = jnp.concatenate([-sin, jnp.zeros((n, HEAD_DIM - ROT_HALF), F32)], axis=1)
    sb = jnp.concatenate([jnp.zeros((n, ROT_HALF), F32), sin, jnp.zeros((n, HEAD_DIM - ROT_DIM), F32)], axis=1)
    return jnp.concatenate([c, sa, sb], axis=1)


def _norm_rope(x, w, c, sa, sb):
    r = lax.rsqrt(jnp.mean(x * x, axis=-1, keepdims=True) + EPS)
    xn = (x * r) * w
    up = pltpu.roll(xn, HEAD_DIM - ROT_HALF, 1)
    dn = pltpu.roll(xn, ROT_HALF, 1)
    return xn * c + up * sa + dn * sb


def _group_weights(l0, l1, l2):
    mx = jnp.maximum(jnp.maximum(l0, l1), l2)
    e0, e1, e2 = jnp.exp(l0 - mx), jnp.exp(l1 - mx), jnp.exp(l2 - mx)
    den = (e0 + e1) + e2
    return e0 / den, e1 / den, e2 / den


def _attn_prompt_body(*refs):
    qkv_refs = refs[:3 * N_GROUPS]
    (rope_ref, qw_ref, kw_ref, o_ref, kv_ref, kbuf, vbuf, qbuf, obuf, lbuf) = refs[3 * N_GROUPS:]
    sb = pl.program_id(2)
    ch = pl.program_id(3)
    n_chunks = pl.num_programs(3)

    @pl.when(jnp.logical_and(sb == 0, ch == 0))
    def _no_previous_superblock():
        kbuf[:, 0:SUPER, :] = jnp.zeros((N_GROUPS, SUPER, HEAD_DIM), F32)
        vbuf[:, 0:SUPER, :] = jnp.zeros((N_GROUPS, SUPER, HEAD_DIM), F32)

    c = rope_ref[:, 0:HEAD_DIM]
    sa = rope_ref[:, HEAD_DIM:2 * HEAD_DIM]
    sbn = rope_ref[:, 2 * HEAD_DIM:3 * HEAD_DIM]
    r0 = pl.multiple_of(ch * CHUNK, CHUNK)
    for g in range(N_GROUPS):
        q_ref, k_ref, v_ref = qkv_refs[3 * g:3 * g + 3]
        qn = _norm_rope(q_ref[...], qw_ref[g:g + 1, :], c, sa, sbn)
        kn = _norm_rope(k_ref[...], kw_ref[g:g + 1, :], c, sa, sbn)
        v = v_ref[...]
        qbuf[g, pl.ds(r0, CHUNK), :] = qn
        kbuf[g, pl.ds(SUPER + r0, CHUNK), :] = kn
        vbuf[g, pl.ds(SUPER + r0, CHUNK), :] = v
        kv_ref[g, 0] = kn
        kv_ref[g, 1] = v

    @pl.when(ch == n_chunks - 1)
    def _attend():
        qi = lax.broadcasted_iota(jnp.int32, (BAND, 2 * BAND), 0)
        kj = lax.broadcasted_iota(jnp.int32, (BAND, 2 * BAND), 1)
        hi = qi + BAND
        lo_first = jnp.maximum(qi, jnp.where(sb == 0, BAND, 0))
        nt = (((1,), (1,)), ((), ()))
        for g, (_, dil) in enumerate(GROUPS):
            for res in range(dil):
                for blk in range(SUPER // (BAND * dil)):
                    start = res + dil * BAND * blk
                    q_rows = pl.ds(start, BAND, stride=dil) if dil > 1 else pl.ds(start, BAND)
                    k_start = SUPER + start - dil * BAND
                    k_rows = pl.ds(k_start, 2 * BAND, stride=dil) if dil > 1 else pl.ds(k_start, 2 * BAND)
                    q = qbuf[g, q_rows, :].astype(BF16)
                    kc = kbuf[g, k_rows, :].astype(BF16)
                    vc = vbuf[g, k_rows, :].astype(BF16)
                    s = lax.dot_general(q, kc, nt, preferred_element_type=F32) * SCALE
                    lo = lo_first if blk == 0 else qi
                    s = jnp.where(kj >= lo, jnp.where(kj <= hi, s, NEG_INF), NEG_INF)
                    m = jnp.max(s, axis=-1, keepdims=True)
                    p = jnp.exp(s - m)
                    l = jnp.sum(p, axis=-1, keepdims=True)
                    o = jnp.dot((p / l).astype(BF16), vc, preferred_element_type=F32)
                    obuf[g, q_rows, :] = o
                    lbuf[g, q_rows, :] = jnp.broadcast_to(m + jnp.log(l), (BAND, HEAD_DIM))

        rows_per_step = 256

        def combine(i, carry):
            rows = pl.ds(pl.multiple_of(i * rows_per_step, rows_per_step), rows_per_step)
            w0, w1, w2 = _group_weights(lbuf[0, rows, :], lbuf[1, rows, :], lbuf[2, rows, :])
            o_ref[rows, :] = (w0 * obuf[0, rows, :] + w1 * obuf[1, rows, :]) + w2 * obuf[2, rows, :]
            cur = pl.ds(pl.multiple_of(SUPER + i * rows_per_step, rows_per_step), rows_per_step)
            for g in range(N_GROUPS):
                kbuf[g, rows, :] = kbuf[g, cur, :]
                vbuf[g, rows, :] = vbuf[g, cur, :]
            return carry
        lax.fori_loop(0, SUPER // rows_per_step, combine, 0)


def _attn_prompt(z, rope, qw, kw, batch, seq):
    assert seq % SUPER == 0 and SUPER % CHUNK == 0
    nsb = seq // SUPER
    nch = SUPER // CHUNK

    def zspec(g, which):
        return pl.BlockSpec(
            (CHUNK, HEAD_DIM),
            lambda b, h, sb, ch: ((b * nsb + sb) * nch + ch, (3 * g + which) * N_HEADS + h))

    in_specs = [zspec(g, which) for g in range(N_GROUPS) for which in range(3)]
    in_specs += [
        pl.BlockSpec((CHUNK, 3 * HEAD_DIM), lambda b, h, sb, ch: (sb * nch + ch, 0)),
        pl.BlockSpec((N_GROUPS, HEAD_DIM), lambda b, h, sb, ch: (0, 0)),
        pl.BlockSpec((N_GROUPS, HEAD_DIM), lambda b, h, sb, ch: (0, 0)),
    ]
    est = (2 * (9 * CHUNK * HEAD_DIM + CHUNK * 3 * HEAD_DIM + SUPER * HEAD_DIM + 6 * CHUNK * HEAD_DIM) * 4
           + N_GROUPS * (2 * 2 * SUPER + 3 * SUPER) * HEAD_DIM * 4 + (12 << 20))
    return pl.pallas_call(
        _attn_prompt_body,
        grid=(batch, N_HEADS, nsb, nch),
        in_specs=in_specs,
        out_specs=[
            pl.BlockSpec((SUPER, HEAD_DIM), lambda b, h, sb, ch: (b * nsb + sb, h)),
            pl.BlockSpec((None, None, N_GROUPS, 2, CHUNK, HEAD_DIM),
                         lambda b, h, sb, ch: (b, h, 0, 0, jnp.where(sb == nsb - 1, ch, 0), 0)),
        ],
        out_shape=[
            jax.ShapeDtypeStruct((batch * seq, D_ATT), F32),
            jax.ShapeDtypeStruct((batch, N_HEADS, N_GROUPS, 2, SUPER, HEAD_DIM), F32),
        ],
        scratch_shapes=[
            pltpu.VMEM((N_GROUPS, 2 * SUPER, HEAD_DIM), F32),
            pltpu.VMEM((N_GROUPS, 2 * SUPER, HEAD_DIM), F32),
            pltpu.VMEM((N_GROUPS, SUPER, HEAD_DIM), F32),
            pltpu.VMEM((N_GROUPS, SUPER, HEAD_DIM), F32),
            pltpu.VMEM((N_GROUPS, SUPER, HEAD_DIM), F32),
        ],
        compiler_params=pltpu.CompilerParams(
            dimension_semantics=("arbitrary", "arbitrary", "arbitrary", "arbitrary"),
            vmem_limit_bytes=_vmem_limit(est)),
        name="attn_prompt",
    )(*([z] * (3 * N_GROUPS)), rope, qw, kw)


def _attn_sample_body(qkv_ref, c0_ref, c1_ref, c2_ref, rope_ref, qw_ref, kw_ref,
                      o_ref, n0_ref, n1_ref, n2_ref):
    t_new = qkv_ref.shape[0]
    cache_refs = (c0_ref, c1_ref, c2_ref)
    new_refs = (n0_ref, n1_ref, n2_ref)
    outs = [[None] * N_GROUPS for _ in range(t_new)]
    lses = [[None] * N_GROUPS for _ in range(t_new)]
    for g, (_, dil) in enumerate(GROUPS):
        qs, ks, vs = [], [], []
        for t in range(t_new):
            c = rope_ref[t:t + 1, 0:HEAD_DIM]
            sa = rope_ref[t:t + 1, HEAD_DIM:2 * HEAD_DIM]
            sbn = rope_ref[t:t + 1, 2 * HEAD_DIM:3 * HEAD_DIM]
            qs.append(_norm_rope(qkv_ref[t, 3 * g], qw_ref[g:g + 1, :], c, sa, sbn))
            ks.append(_norm_rope(qkv_ref[t, 3 * g + 1], kw_ref[g:g + 1, :], c, sa, sbn))
            vs.append(qkv_ref[t, 3 * g + 2])
            new_refs[g][t, 0:N_HEADS, :] = ks[t]
            new_refs[g][t, N_HEADS:2 * N_HEADS, :] = vs[t]
        for t in range(t_new):
            res = t % dil
            first = -((res - t) // dil)
            kc = cache_refs[g][first:, res, 0:N_HEADS, :]
            vc = cache_refs[g][first:, res, N_HEADS:2 * N_HEADS, :]
            q = qs[t]
            s_c = jnp.sum(kc * q[None], axis=-1, keepdims=True) * SCALE
            new_keys = [tp for tp in range(t + 1) if (t - tp) % dil == 0]
            s_n = [jnp.sum(ks[tp] * q, axis=-1, keepdims=True) * SCALE for tp in new_keys]
            m = jnp.max(s_c, axis=0)
            for s in s_n:
                m = jnp.maximum(m, s)
            p_c = jnp.exp(s_c - m[None])
            p_n = [jnp.exp(s - m) for s in s_n]
            l = jnp.sum(p_c, axis=0)
            for p in p_n:
                l = l + p
            o = jnp.sum((p_c / l[None]) * vc, axis=0)
            for p, tp in zip(p_n, new_keys):
                o = o + (p / l) * vs[tp]
            outs[t][g] = o
            lses[t][g] = m + jnp.log(l)
    for t in range(t_new):
        w0, w1, w2 = _group_weights(*lses[t])
        o_ref[t] = (w0 * outs[t][0] + w1 * outs[t][1]) + w2 * outs[t][2]


def _attn_sample(qkv, caches, rope, qw, kw):
    db, t_new = qkv.shape[:2]
    views, cache_specs, cache_bytes = [], [], 0
    for (window, dil), cache in zip(GROUPS, caches):
        assert cache.shape[1] == window == BAND * dil and t_new <= BAND
        n_res = min(dil, t_new)
        views.append(cache.reshape(db, BAND, dil, 2 * N_HEADS, HEAD_DIM))
        cache_specs.append(pl.BlockSpec((None, BAND, n_res, 2 * N_HEADS, HEAD_DIM), lambda b: (b, 0, 0, 0, 0)))
        cache_bytes += BAND * n_res * 2 * N_HEADS * HEAD_DIM * 4
    new_spec = pl.BlockSpec((None, t_new, 2 * N_HEADS, HEAD_DIM), lambda b: (b, 0, 0, 0))
    new_shape = jax.ShapeDtypeStruct((db, t_new, 2 * N_HEADS, HEAD_DIM), F32)
    return pl.pallas_call(
        _attn_sample_body,
        grid=(db,),
        in_specs=[
            pl.BlockSpec((None, t_new, 3 * N_GROUPS, N_HEADS, HEAD_DIM), lambda b: (b, 0, 0, 0, 0)),
            *cache_specs,
            pl.BlockSpec((t_new, 3 * HEAD_DIM), lambda b: (0, 0)),
            pl.BlockSpec((N_GROUPS, HEAD_DIM), lambda b: (0, 0)),
            pl.BlockSpec((N_GROUPS, HEAD_DIM), lambda b: (0, 0)),
        ],
        out_specs=[pl.BlockSpec((None, t_new, N_HEADS, HEAD_DIM), lambda b: (b, 0, 0, 0)),
                   new_spec, new_spec, new_spec],
        out_shape=[jax.ShapeDtypeStruct((db, t_new, N_HEADS, HEAD_DIM), F32),
                   new_shape, new_shape, new_shape],
        compiler_params=pltpu.CompilerParams(
            dimension_semantics=("arbitrary",), vmem_limit_bytes=_vmem_limit(2 * cache_bytes + (16 << 20))),
        name="attn_sample",
    )(qkv, *views, rope, qw, kw)


def _sigmoid(x):
    return 1.0 / (1.0 + jnp.exp(-x))


def _merge_body(*refs, sample, rows_per_seq):
    if sample:
        (oa_ref, ag_ref, hc_ref, bc_ref, cc_ref, cg_ref, mla_ref, mlc_ref, x_ref, cw_ref,
         wa_ref, wc_ref, wo_ref, p1_ref, p2_ref, y_ref, u_ref, ubuf_ref) = refs
    else:
        (oa_ref, ag_ref, hc_ref, bc_ref, cc_ref, cg_ref, mla_ref, mlc_ref, x_ref, cw_ref,
         wa_ref, wc_ref, wo_ref, y_ref, u_ref, ubuf_ref) = refs
    tm = x_ref.shape[0]

    ag = ag_ref[...]
    ain = (oa_ref[...] * (ag * _sigmoid(ag))).astype(BF16)
    a = jnp.dot(ain, wa_ref[...], preferred_element_type=F32)

    u = cc_ref[...] * hc_ref[...]
    if sample:
        ubuf_ref[0:SUBLANES, :] = jnp.zeros((SUBLANES, u.shape[1]), F32)
    else:
        @pl.when(pl.program_id(0) % (rows_per_seq // tm) == 0)
        def _sequence_start():
            ubuf_ref[0:SUBLANES, :] = jnp.zeros((SUBLANES, u.shape[1]), F32)
    ubuf_ref[SUBLANES:SUBLANES + tm, :] = u
    u1 = ubuf_ref[SUBLANES - 1:SUBLANES - 1 + tm, :]
    u2 = ubuf_ref[SUBLANES - 2:SUBLANES - 2 + tm, :]
    if sample:
        t = lax.broadcasted_iota(jnp.int32, (tm, 1), 0) % rows_per_seq
        u1 = jnp.where(t < 1, p1_ref[...], u1)
        u2 = jnp.where(t < 2, p2_ref[...], u2)
        u_ref[...] = u
    else:
        tail = ubuf_ref[tm:tm + SUBLANES, :]
        ubuf_ref[0:SUBLANES, :] = tail
        u_ref[...] = tail
    conv = (cw_ref[0:1, :] * u2 + cw_ref[1:2, :] * u1) + cw_ref[2:3, :] * u
    cg = cg_ref[...]
    cin = ((bc_ref[...] * conv) * (cg * _sigmoid(cg))).astype(BF16)
    cbr = jnp.dot(cin, wc_ref[...], preferred_element_type=F32)

    m = _sigmoid(mla_ref[...]) * a + _sigmoid(mlc_ref[...]) * cbr
    y_ref[...] = x_ref[...] + jnp.dot(m.astype(BF16), wo_ref[...], preferred_element_type=F32)


def _merge(x, z, o_att, conv_w, wa, wc, wo, *, tm, rows_per_seq, state=None):
    m_rows, d = x.shape
    n = z.shape[1]
    sample = state is not None
    assert m_rows % tm == 0 and (sample or rows_per_seq % tm == 0)
    assert n == 3 * N_GROUPS * D_ATT + D_ATT + 6 * d and d % D_ATT == 0
    zb = (3 * N_GROUPS * D_ATT + D_ATT) // d
    row = lambda i: (i, 0)
    const = lambda i: (0, 0)
    zcol = lambda j: (lambda i: (i, j))
    resident = dict(pipeline_mode=pl.Buffered(1))
    in_specs = [
        pl.BlockSpec((tm, D_ATT), row),
        pl.BlockSpec((tm, D_ATT), zcol(3 * N_GROUPS)),
        pl.BlockSpec((tm, d), zcol(zb)),
        pl.BlockSpec((tm, d), zcol(zb + 1)),
        pl.BlockSpec((tm, d), zcol(zb + 2)),
        pl.BlockSpec((tm, d), zcol(zb + 3)),
        pl.BlockSpec((tm, d), zcol(zb + 4)),
        pl.BlockSpec((tm, d), zcol(zb + 5)),
        pl.BlockSpec((tm, d), row),
        pl.BlockSpec((CONV_WIDTH, d), const),
        pl.BlockSpec((D_ATT, d), const, **resident),
        pl.BlockSpec((d, d), const, **resident),
        pl.BlockSpec((d, d), const, **resident),
    ]
    args = [o_att, z, z, z, z, z, z, z, x, conv_w, wa, wc, wo]
    if sample:
        in_specs += [pl.BlockSpec((tm, d), row), pl.BlockSpec((tm, d), row)]
        args += list(state)
        u_spec = pl.BlockSpec((tm, d), row)
        u_shape = jax.ShapeDtypeStruct((m_rows, d), F32)
    else:
        per_seq = rows_per_seq // tm
        u_spec = pl.BlockSpec((None, SUBLANES, d), lambda i: (i // per_seq, 0, 0))
        u_shape = jax.ShapeDtypeStruct((m_rows // rows_per_seq, SUBLANES, d), F32)
    est = (2 * tm * (2 * D_ATT + 8 * d) * 4 + (D_ATT * d + 2 * d * d) * 2
           + 10 * tm * d * 4 + (4 << 20))
    return pl.pallas_call(
        functools.partial(_merge_body, sample=sample, rows_per_seq=rows_per_seq),
        grid=(m_rows // tm,),
        in_specs=in_specs,
        out_specs=[pl.BlockSpec((tm, d), row), u_spec],
        out_shape=[jax.ShapeDtypeStruct((m_rows, d), F32), u_shape],
        scratch_shapes=[pltpu.VMEM((tm + 2 * SUBLANES, d), F32)],
        compiler_params=pltpu.CompilerParams(
            dimension_semantics=("arbitrary",), vmem_limit_bytes=_vmem_limit(est)),
        name="merge_sample" if sample else "merge_prompt",
    )(*args)


def _cache_roll_body(*refs, n_caches, n_split):
    caches = refs[:n_caches]
    news = refs[n_caches:2 * n_caches]
    outs = refs[2 * n_caches:3 * n_caches]
    sem = refs[3 * n_caches]
    copies = []
    for i in range(n_caches):
        db, window = caches[i].shape[:2]
        t_new = news[i].shape[1]
        keep = window - t_new
        per = db // n_split
        for j in range(n_split):
            bs = pl.ds(j * per, per)
            copies.append(pltpu.make_async_copy(
                caches[i].at[bs, pl.ds(t_new, keep)], outs[i].at[bs, pl.ds(0, keep)], sem.at[i, j]))
        copies.append(pltpu.make_async_copy(news[i], outs[i].at[:, pl.ds(keep, t_new)], sem.at[i, n_split]))
    for cp in copies:
        cp.start()
    for cp in copies:
        cp.wait()


def _cache_roll(caches, news, n_split=4):
    n = len(caches)
    assert all(c.shape[0] % n_split == 0 for c in caches)
    any_spec = pl.BlockSpec(memory_space=pl.ANY)
    return pl.pallas_call(
        functools.partial(_cache_roll_body, n_caches=n, n_split=n_split),
        in_specs=[any_spec] * (2 * n),
        out_specs=[any_spec] * n,
        out_shape=[jax.ShapeDtypeStruct(c.shape, c.dtype) for c in caches],
        scratch_shapes=[pltpu.SemaphoreType.DMA((n, n_split + 1))],
        name="cache_roll",
    )(*caches, *news)


def _layer(xp, xs, caches, state, norm_w, w_in, q_norm_w, k_norm_w, conv_w, w_att_proj, w_conv_proj, w_out):
    b, s, d = xp.shape
    db, t_new, _ = xs.shape
    w_in_b = w_in.astype(BF16)
    wa, wc, wo = w_att_proj.astype(BF16), w_conv_proj.astype(BF16), w_out.astype(BF16)
    g = norm_w.reshape(1, d)

    zp = _proj(xp.reshape(b * s, d), g, w_in_b, tm=1024, tn=1024)
    rope_p = _rope_tables(jnp.arange(s, dtype=F32))
    o_att_p, kv_tail = _attn_prompt(zp, rope_p, q_norm_w, k_norm_w, b, s)
    kv_p = []
    for gi, (window, dil) in enumerate(GROUPS):
        assert window == BAND * dil and window <= SUPER
        tail = kv_tail[:, :, gi, :, SUPER - window:, :]
        kv_p.append(jnp.transpose(tail, (0, 3, 2, 1, 4)))
    yp, u_tail = _merge(xp.reshape(b * s, d), zp, o_att_p, conv_w, wa, wc, wo, tm=128, rows_per_seq=s)
    conv_p = u_tail[:, SUBLANES - (CONV_WIDTH - 1):]

    rows_s = db * t_new
    zs = _proj(xs.reshape(rows_s, d), g, w_in_b, tm=rows_s, tn=1024)
    rope_s = _rope_tables(PAST_LEN + jnp.arange(t_new, dtype=F32))
    qkv_s = zs[:, :3 * N_GROUPS * D_ATT].reshape(db, t_new, 3 * N_GROUPS, N_HEADS, HEAD_DIM)
    flat_caches = [c.reshape(db, c.shape[1], 2 * N_HEADS, HEAD_DIM) for c in caches]
    o_att_s, *kv_new = _attn_sample(qkv_s, flat_caches, rope_s, q_norm_w, k_norm_w)
    p1 = jnp.pad(state[:, 1:2], ((0, 0), (0, t_new - 1), (0, 0))).reshape(rows_s, d)
    p2 = jnp.pad(state, ((0, 0), (0, t_new - 2), (0, 0))).reshape(rows_s, d)
    ys, u_s = _merge(xs.reshape(rows_s, d), zs, o_att_s.reshape(rows_s, D_ATT), conv_w, wa, wc, wo,
                     tm=rows_s, rows_per_seq=t_new, state=(p1, p2))
    conv_s = u_s.reshape(db, t_new, d)[:, t_new - (CONV_WIDTH - 1):]
    kv_s = [c.reshape(db, -1, 2, N_HEADS, HEAD_DIM) for c in _cache_roll(flat_caches, kv_new)]

    return yp.reshape(b, s, d), ys.reshape(db, t_new, d), kv_p, conv_p, kv_s, conv_s


def kernel(x_prompt, x_sample, cache_kv_w128, cache_kv_w512, cache_kv_w2048, state_conv, norm_w, w_in, q_norm_w,
           k_norm_w, conv_w, w_att_proj, w_conv_proj, w_out):
    depth = norm_w.shape[0]
    caches = (cache_kv_w128, cache_kv_w512, cache_kv_w2048)
    yp, ys = x_prompt, x_sample
    kv_p = [[] for _ in GROUPS]
    kv_s = [[] for _ in GROUPS]
    conv_p, conv_s = [], []
    for l in range(depth):
        yp, ys, kvp_l, cp_l, kvs_l, cs_l = _layer(
            yp, ys, [c[l] for c in caches], state_conv[l], norm_w[l], w_in[l], q_norm_w[l], k_norm_w[l],
            conv_w[l], w_att_proj[l], w_conv_proj[l], w_out[l])
        for gi in range(N_GROUPS):
            kv_p[gi].append(kvp_l[gi])
            kv_s[gi].append(kvs_l[gi])
        conv_p.append(cp_l)
        conv_s.append(cs_l)
    stack = lambda xs: xs[0][None] if len(xs) == 1 else jnp.stack(xs, axis=0)
    return (yp, ys, stack(kv_p[0]), stack(kv_p[1]), stack(kv_p[2]), stack(conv_p),
            stack(kv_s[0]), stack(kv_s[1]), stack(kv_s[2]), stack(conv_s))
```

```python
import functools

import jax
import jax.numpy as jnp
from jax import lax
from jax.experimental import pallas as pl
from jax.experimental.pallas import tpu as pltpu

F32 = jnp.float32
BF16 = jnp.bfloat16

HEAD_DIM = 128
N_HEADS = 8
GROUPS = ((128, 1), (512, 4), (2048, 16))
N_GROUPS = len(GROUPS)
BAND = 128
D_ATT = N_HEADS * HEAD_DIM
ROT_DIM = HEAD_DIM // 4
ROT_HALF = ROT_DIM // 2
ROPE_THETA = 500000.0
EPS = 1e-6
SCALE = HEAD_DIM ** -0.5
NEG_INF = -1e30
PAST_LEN = 16384
CONV_WIDTH = 3

V7X_VMEM_BYTES = 64 * 1024 * 1024
SUBLANES = 8

SUPER = BAND * max(d for _, d in GROUPS)
CHUNK = 512


def _vmem_limit(estimate_bytes):
    return int(min(V7X_VMEM_BYTES - 8 * 1024 * 1024, max(32 * 1024 * 1024, estimate_bytes)))


def _proj_body(x_ref, g_ref, w_ref, o_ref, xn_ref, *, chunk):
    tm = x_ref.shape[0]

    @pl.when(pl.program_id(1) == 0)
    def _normalise():
        def step(c, carry):
            r0 = pl.multiple_of(c * chunk, chunk)
            x = x_ref[pl.ds(r0, chunk), :]
            ms = jnp.mean(x * x, axis=-1, keepdims=True)
            xn_ref[pl.ds(r0, chunk), :] = ((x * lax.rsqrt(ms + EPS)) * g_ref[...]).astype(BF16)
            return carry
        lax.fori_loop(0, tm // chunk, step, 0)

    o_ref[...] = jnp.dot(xn_ref[...], w_ref[...], preferred_element_type=F32)


def _proj(x, g, w_bf16, *, tm, tn):
    m, d = x.shape
    n = w_bf16.shape[1]
    assert m % tm == 0 and n % tn == 0
    chunk = min(tm, 128)
    est = 2 * tm * d * 4 + tm * d * 2 + 2 * d * tn * 2 + 2 * tm * tn * 4 + (4 << 20)
    return pl.pallas_call(
        functools.partial(_proj_body, chunk=chunk),
        grid=(m // tm, n // tn),
        in_specs=[
            pl.BlockSpec((tm, d), lambda i, j: (i, 0)),
            pl.BlockSpec((1, d), lambda i, j: (0, 0)),
            pl.BlockSpec((d, tn), lambda i, j: (0, j)),
        ],
        out_specs=pl.BlockSpec((tm, tn), lambda i, j: (i, j)),
        out_shape=jax.ShapeDtypeStruct((m, n), F32),
        scratch_shapes=[pltpu.VMEM((tm, d), BF16)],
        compiler_params=pltpu.CompilerParams(
            dimension_semantics=("arbitrary", "arbitrary"),
            vmem_limit_bytes=_vmem_limit(est)),
        name="proj",
    )(x, g, w_bf16)


def _rope_tables(pos):
    n = pos.shape[0]
    inv = ROPE_THETA ** (-jnp.arange(ROT_HALF, dtype=F32) * (2.0 / ROT_DIM))
    ang = pos[:, None] * inv[None, :]
    cos, sin = jnp.cos(ang), jnp.sin(ang)
    c = jnp.concatenate([cos, cos, jnp.ones((n, HEAD_DIM - ROT_DIM), F32)], axis=1)
    sa = jnp.concatenate([-sin, jnp.zeros((n, HEAD_DIM - ROT_HALF), F32)], axis=1)
    sb = jnp.concatenate([jnp.zeros((n, ROT_HALF), F32), sin, jnp.zeros((n, HEAD_DIM - ROT_DIM), F32)], axis=1)
    return jnp.concatenate([c, sa, sb], axis=1)


def _norm_rope(x, w, c, sa, sb):
    r = lax.rsqrt(jnp.mean(x * x, axis=-1, keepdims=True) + EPS)
    xn = (x * r) * w
    up = pltpu.roll(xn, HEAD_DIM - ROT_HALF, 1)
    dn = pltpu.roll(xn, ROT_HALF, 1)
    return xn * c + up * sa + dn * sb


def _group_weights(l0, l1, l2):
    mx = jnp.maximum(jnp.maximum(l0, l1), l2)
    e0, e1, e2 = jnp.exp(l0 - mx), jnp.exp(l1 - mx), jnp.exp(l2 - mx)
    den = (e0 + e1) + e2
    return e0 / den, e1 / den, e2 / den


def _attn_prompt_body(*refs):
    qkv_refs = refs[:3 * N_GROUPS]
    (rope_ref, qw_ref, kw_ref, o_ref, kv_ref, kbuf, vbuf, qbuf, obuf, lbuf) = refs[3 * N_GROUPS:]
    sb = pl.program_id(2)
    ch = pl.program_id(3)
    n_chunks = pl.num_programs(3)

    @pl.when(jnp.logical_and(sb == 0, ch == 0))
    def _no_previous_superblock():
        kbuf[:, 0:SUPER, :] = jnp.zeros((N_GROUPS, SUPER, HEAD_DIM), F32)
        vbuf[:, 0:SUPER, :] = jnp.zeros((N_GROUPS, SUPER, HEAD_DIM), F32)

    c = rope_ref[:, 0:HEAD_DIM]
    sa = rope_ref[:, HEAD_DIM:2 * HEAD_DIM]
    sbn = rope_ref[:, 2 * HEAD_DIM:3 * HEAD_DIM]
    r0 = pl.multiple_of(ch * CHUNK, CHUNK)
    for g in range(N_GROUPS):
        q_ref, k_ref, v_ref = qkv_refs[3 * g:3 * g + 3]
        qn = _norm_rope(q_ref[...], qw_ref[g:g + 1, :], c, sa, sbn)
        kn = _norm_rope(k_ref[...], kw_ref[g:g + 1, :], c, sa, sbn)
        v = v_ref[...]
        qbuf[g, pl.ds(r0, CHUNK), :] = qn
        kbuf[g, pl.ds(SUPER + r0, CHUNK), :] = kn
        vbuf[g, pl.ds(SUPER + r0, CHUNK), :] = v
        kv_ref[g, 0] = kn
        kv_ref[g, 1] = v

    @pl.when(ch == n_chunks - 1)
    def _attend():
        qi = lax.broadcasted_iota(jnp.int32, (BAND, 2 * BAND), 0)
        kj = lax.broadcasted_iota(jnp.int32, (BAND, 2 * BAND), 1)
        hi = qi + BAND
        lo_first = jnp.maximum(qi, jnp.where(sb == 0, BAND, 0))
        nt = (((1,), (1,)), ((), ()))
        for g, (_, dil) in enumerate(GROUPS):
            for res in range(dil):
                for blk in range(SUPER // (BAND * dil)):
                    start = res + dil * BAND * blk
                    q_rows = pl.ds(start, BAND, stride=dil) if dil > 1 else pl.ds(start, BAND)
                    k_start = SUPER + start - dil * BAND
                    k_rows = pl.ds(k_start, 2 * BAND, stride=dil) if dil > 1 else pl.ds(k_start, 2 * BAND)
                    q = qbuf[g, q_rows, :].astype(BF16)
                    kc = kbuf[g, k_rows, :].astype(BF16)
                    vc = vbuf[g, k_rows, :].astype(BF16)
                    s = lax.dot_general(q, kc, nt, preferred_element_type=F32) * SCALE
                    lo = lo_first if blk == 0 else qi
                    s = jnp.where(kj >= lo, jnp.where(kj <= hi, s, NEG_INF), NEG_INF)
                    m = jnp.max(s, axis=-1, keepdims=True)
                    p = jnp.exp(s - m)
                    l = jnp.sum(p, axis=-1, keepdims=True)
                    o = jnp.dot((p / l).astype(BF16), vc, preferred_element_type=F32)
                    obuf[g, q_rows, :] = o
                    lbuf[g, q_rows, :] = jnp.broadcast_to(m + jnp.log(l), (BAND, HEAD_DIM))

        rows_per_step = 256

        def combine(i, carry):
            rows = pl.ds(pl.multiple_of(i * rows_per_step, rows_per_step), rows_per_step)
            w0, w1, w2 = _group_weights(lbuf[0, rows, :], lbuf[1, rows, :], lbuf[2, rows, :])
            o_ref[rows, :] = (w0 * obuf[0, rows, :] + w1 * obuf[1, rows, :]) + w2 * obuf[2, rows, :]
            cur = pl.ds(pl.multiple_of(SUPER + i * rows_per_step, rows_per_step), rows_per_step)
            for g in range(N_GROUPS):
                kbuf[g, rows, :] = kbuf[g, cur, :]
                vbuf[g, rows, :] = vbuf[g, cur, :]
            return carry
        lax.fori_loop(0, SUPER // rows_per_step, combine, 0)


def _attn_prompt(z, rope, qw, kw, batch, seq):
    assert seq % SUPER == 0 and SUPER % CHUNK == 0
    nsb = seq // SUPER
    nch = SUPER // CHUNK

    def zspec(g, which):
        return pl.BlockSpec(
            (CHUNK, HEAD_DIM),
            lambda b, h, sb, ch: ((b * nsb + sb) * nch + ch, (3 * g + which) * N_HEADS + h))

    in_specs = [zspec(g, which) for g in range(N_GROUPS) for which in range(3)]
    in_specs += [
        pl.BlockSpec((CHUNK, 3 * HEAD_DIM), lambda b, h, sb, ch: (sb * nch + ch, 0)),
        pl.BlockSpec((N_GROUPS, HEAD_DIM), lambda b, h, sb, ch: (0, 0)),
        pl.BlockSpec((N_GROUPS, HEAD_DIM), lambda b, h, sb, ch: (0, 0)),
    ]
    est = (2 * (9 * CHUNK * HEAD_DIM + CHUNK * 3 * HEAD_DIM + SUPER * HEAD_DIM + 6 * CHUNK * HEAD_DIM) * 4
           + N_GROUPS * (2 * 2 * SUPER + 3 * SUPER) * HEAD_DIM * 4 + (12 << 20))
    return pl.pallas_call(
        _attn_prompt_body,
        grid=(batch, N_HEADS, nsb, nch),
        in_specs=in_specs,
        out_specs=[
            pl.BlockSpec((SUPER, HEAD_DIM), lambda b, h, sb, ch: (b * nsb + sb, h)),
            pl.BlockSpec((None, None, N_GROUPS, 2, CHUNK, HEAD_DIM),
                         lambda b, h, sb, ch: (b, h, 0, 0, jnp.where(sb == nsb - 1, ch, 0), 0)),
        ],
        out_shape=[
            jax.ShapeDtypeStruct((batch * seq, D_ATT), F32),
            jax.ShapeDtypeStruct((batch, N_HEADS, N_GROUPS, 2, SUPER, HEAD_DIM), F32),
        ],
        scratch_shapes=[
            pltpu.VMEM((N_GROUPS, 2 * SUPER, HEAD_DIM), F32),
            pltpu.VMEM((N_GROUPS, 2 * SUPER, HEAD_DIM), F32),
            pltpu.VMEM((N_GROUPS, SUPER, HEAD_DIM), F32),
            pltpu.VMEM((N_GROUPS, SUPER, HEAD_DIM), F32),
            pltpu.VMEM((N_GROUPS, SUPER, HEAD_DIM), F32),
        ],
        compiler_params=pltpu.CompilerParams(
            dimension_semantics=("arbitrary", "arbitrary", "arbitrary", "arbitrary"),
            vmem_limit_bytes=_vmem_limit(est)),
        name="attn_prompt",
    )(*([z] * (3 * N_GROUPS)), rope, qw, kw)


def _attn_sample_body(qkv_ref, c0_ref, c1_ref, c2_ref, rope_ref, qw_ref, kw_ref,
                      o_ref, n0_ref, n1_ref, n2_ref):
    t_new = qkv_ref.shape[0]
    cache_refs = (c0_ref, c1_ref, c2_ref)
    new_refs = (n0_ref, n1_ref, n2_ref)
    outs = [[None] * N_GROUPS for _ in range(t_new)]
    lses = [[None] * N_GROUPS for _ in range(t_new)]
    for g, (_, dil) in enumerate(GROUPS):
        qs, ks, vs = [], [], []
        for t in range(t_new):
            c = rope_ref[t:t + 1, 0:HEAD_DIM]
            sa = rope_ref[t:t + 1, HEAD_DIM:2 * HEAD_DIM]
            sbn = rope_ref[t:t + 1, 2 * HEAD_DIM:3 * HEAD_DIM]
            qs.append(_norm_rope(qkv_ref[t, 3 * g], qw_ref[g:g + 1, :], c, sa, sbn))
            ks.append(_norm_rope(qkv_ref[t, 3 * g + 1], kw_ref[g:g + 1, :], c, sa, sbn))
            vs.append(qkv_ref[t, 3 * g + 2])
            new_refs[g][t, 0:N_HEADS, :] = ks[t]
            new_refs[g][t, N_HEADS:2 * N_HEADS, :] = vs[t]
        for t in range(t_new):
            res = t % dil
            first = -((res - t) // dil)
            kc = cache_refs[g][first:, res, 0:N_HEADS, :]
            vc = cache_refs[g][first:, res, N_HEADS:2 * N_HEADS, :]
            q = qs[t]
            s_c = jnp.sum(kc * q[None], axis=-1, keepdims=True) * SCALE
            new_keys = [tp for tp in range(t + 1) if (t - tp) % dil == 0]
            s_n = [jnp.sum(ks[tp] * q, axis=-1, keepdims=True) * SCALE for tp in new_keys]
            m = jnp.max(s_c, axis=0)
            for s in s_n:
                m = jnp.maximum(m, s)
            p_c = jnp.exp(s_c - m[None])
            p_n = [jnp.exp(s - m) for s in s_n]
            l = jnp.sum(p_c, axis=0)
            for p in p_n:
                l = l + p
            o = jnp.sum((p_c / l[None]) * vc, axis=0)
            for p, tp in zip(p_n, new_keys):
                o = o + (p / l) * vs[tp]
            outs[t][g] = o
            lses[t][g] = m + jnp.log(l)
    for t in range(t_new):
        w0, w1, w2 = _group_weights(*lses[t])
        o_ref[t] = (w0 * outs[t][0] + w1 * outs[t][1]) + w2 * outs[t][2]


def _attn_sample(qkv, caches, rope, qw, kw):
    db, t_new = qkv.shape[:2]
    views, cache_specs, cache_bytes = [], [], 0
    for (window, dil), cache in zip(GROUPS, caches):
        assert cache.shape[1] == window == BAND * dil and t_new <= BAND
        n_res = min(dil, t_new)
        views.append(cache.reshape(db, BAND, dil, 2 * N_HEADS, HEAD_DIM))
        cache_specs.append(pl.BlockSpec((None, BAND, n_res, 2 * N_HEADS, HEAD_DIM), lambda b: (b, 0, 0, 0, 0)))
        cache_bytes += BAND * n_res * 2 * N_HEADS * HEAD_DIM * 4
    new_spec = pl.BlockSpec((None, t_new, 2 * N_HEADS, HEAD_DIM), lambda b: (b, 0, 0, 0))
    new_shape = jax.ShapeDtypeStruct((db, t_new, 2 * N_HEADS, HEAD_DIM), F32)
    return pl.pallas_call(
        _attn_sample_body,
        grid=(db,),
        in_specs=[
            pl.BlockSpec((None, t_new, 3 * N_GROUPS, N_HEADS, HEAD_DIM), lambda b: (b, 0, 0, 0, 0)),
            *cache_specs,
            pl.BlockSpec((t_new, 3 * HEAD_DIM), lambda b: (0, 0)),
            pl.BlockSpec((N_GROUPS, HEAD_DIM), lambda b: (0, 0)),
            pl.BlockSpec((N_GROUPS, HEAD_DIM), lambda b: (0, 0)),
        ],
        out_specs=[pl.BlockSpec((None, t_new, N_HEADS, HEAD_DIM), lambda b: (b, 0, 0, 0)),
                   new_spec, new_spec, new_spec],
        out_shape=[jax.ShapeDtypeStruct((db, t_new, N_HEADS, HEAD_DIM), F32),
                   new_shape, new_shape, new_shape],
        compiler_params=pltpu.CompilerParams(
            dimension_semantics=("arbitrary",), vmem_limit_bytes=_vmem_limit(2 * cache_bytes + (16 << 20))),
        name="attn_sample",
    )(qkv, *views, rope, qw, kw)


def _sigmoid(x):
    return 1.0 / (1.0 + jnp.exp(-x))


def _merge_body(*refs, sample, rows_per_seq):
    if sample:
        (oa_ref, ag_ref, hc_ref, bc_ref, cc_ref, cg_ref, mla_ref, mlc_ref, x_ref, cw_ref,
         wa_ref, wc_ref, wo_ref, p1_ref, p2_ref, y_ref, u_ref, ubuf_ref) = refs
    else:
        (oa_ref, ag_ref, hc_ref, bc_ref, cc_ref, cg_ref, mla_ref, mlc_ref, x_ref, cw_ref,
         wa_ref, wc_ref, wo_ref, y_ref, u_ref, ubuf_ref) = refs
    tm = x_ref.shape[0]

    ag = ag_ref[...]
    ain = (oa_ref[...] * (ag * _sigmoid(ag))).astype(BF16)
    a = jnp.dot(ain, wa_ref[...], preferred_element_type=F32)

    u = cc_ref[...] * hc_ref[...]
    if sample:
        ubuf_ref[0:SUBLANES, :] = jnp.zeros((SUBLANES, u.shape[1]), F32)
    else:
        @pl.when(pl.program_id(0) % (rows_per_seq // tm) == 0)
        def _sequence_start():
            ubuf_ref[0:SUBLANES, :] = jnp.zeros((SUBLANES, u.shape[1]), F32)
    ubuf_ref[SUBLANES:SUBLANES + tm, :] = u
    u1 = ubuf_ref[SUBLANES - 1:SUBLANES - 1 + tm, :]
    u2 = ubuf_ref[SUBLANES - 2:SUBLANES - 2 + tm, :]
    if sample:
        t = lax.broadcasted_iota(jnp.int32, (tm, 1), 0) % rows_per_seq
        u1 = jnp.where(t < 1, p1_ref[...], u1)
        u2 = jnp.where(t < 2, p2_ref[...], u2)
        u_ref[...] = u
    else:
        tail = ubuf_ref[tm:tm + SUBLANES, :]
        ubuf_ref[0:SUBLANES, :] = tail
        u_ref[...] = tail
    conv = (cw_ref[0:1, :] * u2 + cw_ref[1:2, :] * u1) + cw_ref[2:3, :] * u
    cg = cg_ref[...]
    cin = ((bc_ref[...] * conv) * (cg * _sigmoid(cg))).astype(BF16)
    cbr = jnp.dot(cin, wc_ref[...], preferred_element_type=F32)

    m = _sigmoid(mla_ref[...]) * a + _sigmoid(mlc_ref[...]) * cbr
    y_ref[...] = x_ref[...] + jnp.dot(m.astype(BF16), wo_ref[...], preferred_element_type=F32)


def _merge(x, z, o_att, conv_w, wa, wc, wo, *, tm, rows_per_seq, state=None):
    m_rows, d = x.shape
    n = z.shape[1]
    sample = state is not None
    assert m_rows % tm == 0 and (sample or rows_per_seq % tm == 0)
    assert n == 3 * N_GROUPS * D_ATT + D_ATT + 6 * d and d % D_ATT == 0
    zb = (3 * N_GROUPS * D_ATT + D_ATT) // d
    row = lambda i: (i, 0)
    const = lambda i: (0, 0)
    zcol = lambda j: (lambda i: (i, j))
    resident = dict(pipeline_mode=pl.Buffered(1))
    in_specs = [
        pl.BlockSpec((tm, D_ATT), row),
        pl.BlockSpec((tm, D_ATT), zcol(3 * N_GROUPS)),
        pl.BlockSpec((tm, d), zcol(zb)),
        pl.BlockSpec((tm, d), zcol(zb + 1)),
        pl.BlockSpec((tm, d), zcol(zb + 2)),
        pl.BlockSpec((tm, d), zcol(zb + 3)),
        pl.BlockSpec((tm, d), zcol(zb + 4)),
        pl.BlockSpec((tm, d), zcol(zb + 5)),
        pl.BlockSpec((tm, d), row),
        pl.BlockSpec((CONV_WIDTH, d), const),
        pl.BlockSpec((D_ATT, d), const, **resident),
        pl.BlockSpec((d, d), const, **resident),
        pl.BlockSpec((d, d), const, **resident),
    ]
    args = [o_att, z, z, z, z, z, z, z, x, conv_w, wa, wc, wo]
    if sample:
        in_specs += [pl.BlockSpec((tm, d), row), pl.BlockSpec((tm, d), row)]
        args += list(state)
        u_spec = pl.BlockSpec((tm, d), row)
        u_shape = jax.ShapeDtypeStruct((m_rows, d), F32)
    else:
        per_seq = rows_per_seq // tm
        u_spec = pl.BlockSpec((None, SUBLANES, d), lambda i: (i // per_seq, 0, 0))
        u_shape = jax.ShapeDtypeStruct((m_rows // rows_per_seq, SUBLANES, d), F32)
    est = (2 * tm * (2 * D_ATT + 8 * d) * 4 + (D_ATT * d + 2 * d * d) * 2
           + 10 * tm * d * 4 + (4 << 20))
    return pl.pallas_call(
        functools.partial(_merge_body, sample=sample, rows_per_seq=rows_per_seq),
        grid=(m_rows // tm,),
        in_specs=in_specs,
        out_specs=[pl.BlockSpec((tm, d), row), u_spec],
        out_shape=[jax.ShapeDtypeStruct((m_rows, d), F32), u_shape],
        scratch_shapes=[pltpu.VMEM((tm + 2 * SUBLANES, d), F32)],
        compiler_params=pltpu.CompilerParams(
            dimension_semantics=("arbitrary",), vmem_limit_bytes=_vmem_limit(est)),
        name="merge_sample" if sample else "merge_prompt",
    )(*args)


ROLL_ROWS = 512
ROLL_BUFFERS = 4
ROLL_LAG = 2


def _cache_roll_body(*refs, n_caches, items):
    caches = refs[:n_caches]
    news = refs[n_caches:2 * n_caches]
    outs = refs[2 * n_caches:3 * n_caches]
    buf, new_buf, sem_in, sem_out, sem_new = refs[3 * n_caches:]

    def new_in(g):
        return pltpu.make_async_copy(news[g], new_buf.at[g], sem_new.at[0, g])

    def new_out(g):
        shift = news[g].shape[1]
        keep = outs[g].shape[1] - shift
        return pltpu.make_async_copy(new_buf.at[g], outs[g].at[:, pl.ds(keep, shift)], sem_new.at[1, g])

    for g in range(n_caches):
        new_in(g).start()

    def copy_in(i):
        g, b, src, _, n = items[i]
        slot = i % ROLL_BUFFERS
        return pltpu.make_async_copy(caches[g].at[b, pl.ds(src, n)], buf.at[slot, pl.ds(0, n)], sem_in.at[slot])

    def copy_out(i):
        g, b, _, dst, n = items[i]
        slot = i % ROLL_BUFFERS
        return pltpu.make_async_copy(buf.at[slot, pl.ds(0, n)], outs[g].at[b, pl.ds(dst, n)], sem_out.at[slot])

    n_items = len(items)
    for i in range(n_items + ROLL_LAG):
        if i < n_items:
            if i >= ROLL_BUFFERS:
                copy_out(i - ROLL_BUFFERS).wait()
            copy_in(i).start()
        k = i - ROLL_LAG
        if 0 <= k < n_items:
            copy_in(k).wait()
            copy_out(k).start()
    for g in range(n_caches):
        new_in(g).wait()
        new_out(g).start()
    for k in range(max(0, n_items - ROLL_BUFFERS), n_items):
        copy_out(k).wait()
    for g in range(n_caches):
        new_out(g).wait()


def _cache_roll(caches, news):
    n = len(caches)
    shift = news[0].shape[1]
    assert all(nw.shape == news[0].shape for nw in news)
    items = []
    for g, c in enumerate(caches):
        db, window = c.shape[:2]
        keep = window - shift
        for b in range(db):
            for r0 in range(0, keep, ROLL_ROWS):
                items.append((g, b, r0 + shift, r0, min(ROLL_ROWS, keep - r0)))
    any_spec = pl.BlockSpec(memory_space=pl.ANY)
    row_shape = caches[0].shape[2:]
    return pl.pallas_call(
        functools.partial(_cache_roll_body, n_caches=n, items=tuple(items)),
        in_specs=[any_spec] * (2 * n),
        out_specs=[any_spec] * n,
        out_shape=[jax.ShapeDtypeStruct(c.shape, c.dtype) for c in caches],
        scratch_shapes=[
            pltpu.VMEM((ROLL_BUFFERS, ROLL_ROWS) + row_shape, F32),
            pltpu.VMEM((n,) + news[0].shape, F32),
            pltpu.SemaphoreType.DMA((ROLL_BUFFERS,)),
            pltpu.SemaphoreType.DMA((ROLL_BUFFERS,)),
            pltpu.SemaphoreType.DMA((2, n)),
        ],
        compiler_params=pltpu.CompilerParams(
            vmem_limit_bytes=_vmem_limit((ROLL_BUFFERS * ROLL_ROWS + n * news[0].shape[0] * shift)
                                         * 2 * N_HEADS * HEAD_DIM * 4 + (8 << 20))),
        name="cache_roll",
    )(*caches, *news)


def _layer(xp, xs, caches, state, norm_w, w_in, q_norm_w, k_norm_w, conv_w, w_att_proj, w_conv_proj, w_out):
    b, s, d = xp.shape
    db, t_new, _ = xs.shape
    w_in_b = w_in.astype(BF16)
    wa, wc, wo = w_att_proj.astype(BF16), w_conv_proj.astype(BF16), w_out.astype(BF16)
    g = norm_w.reshape(1, d)

    zp = _proj(xp.reshape(b * s, d), g, w_in_b, tm=1024, tn=1024)
    rope_p = _rope_tables(jnp.arange(s, dtype=F32))
    o_att_p, kv_tail = _attn_prompt(zp, rope_p, q_norm_w, k_norm_w, b, s)
    kv_p = []
    for gi, (window, dil) in enumerate(GROUPS):
        assert window == BAND * dil and window <= SUPER
        tail = kv_tail[:, :, gi, :, SUPER - window:, :]
        kv_p.append(jnp.transpose(tail, (0, 3, 2, 1, 4)))
    yp, u_tail = _merge(xp.reshape(b * s, d), zp, o_att_p, conv_w, wa, wc, wo, tm=128, rows_per_seq=s)
    conv_p = u_tail[:, SUBLANES - (CONV_WIDTH - 1):]

    rows_s = db * t_new
    zs = _proj(xs.reshape(rows_s, d), g, w_in_b, tm=rows_s, tn=1024)
    rope_s = _rope_tables(PAST_LEN + jnp.arange(t_new, dtype=F32))
    qkv_s = zs[:, :3 * N_GROUPS * D_ATT].reshape(db, t_new, 3 * N_GROUPS, N_HEADS, HEAD_DIM)
    flat_caches = [c.reshape(db, c.shape[1], 2 * N_HEADS, HEAD_DIM) for c in caches]
    o_att_s, *kv_new = _attn_sample(qkv_s, flat_caches, rope_s, q_norm_w, k_norm_w)
    rolled = _cache_roll(flat_caches, kv_new)
    p1 = jnp.pad(state[:, 1:2], ((0, 0), (0, t_new - 1), (0, 0))).reshape(rows_s, d)
    p2 = jnp.pad(state, ((0, 0), (0, t_new - 2), (0, 0))).reshape(rows_s, d)
    ys, u_s = _merge(xs.reshape(rows_s, d), zs, o_att_s.reshape(rows_s, D_ATT), conv_w, wa, wc, wo,
                     tm=rows_s, rows_per_seq=t_new, state=(p1, p2))
    conv_s = u_s.reshape(db, t_new, d)[:, t_new - (CONV_WIDTH - 1):]
    kv_s = [c.reshape(db, -1, 2, N_HEADS, HEAD_DIM) for c in rolled]

    return yp.reshape(b, s, d), ys.reshape(db, t_new, d), kv_p, conv_p, kv_s, conv_s


def kernel(x_prompt, x_sample, cache_kv_w128, cache_kv_w512, cache_kv_w2048, state_conv, norm_w, w_in, q_norm_w,
           k_norm_w, conv_w, w_att_proj, w_conv_proj, w_out):
    depth = norm_w.shape[0]
    caches = (cache_kv_w128, cache_kv_w512, cache_kv_w2048)
    yp, ys = x_prompt, x_sample
    kv_p = [[] for _ in GROUPS]
    kv_s = [[] for _ in GROUPS]
    conv_p, conv_s = [], []
    for l in range(depth):
        yp, ys, kvp_l, cp_l, kvs_l, cs_l = _layer(
            yp, ys, [c[l] for c in caches], state_conv[l], norm_w[l], w_in[l], q_norm_w[l], k_norm_w[l],
            conv_w[l], w_att_proj[l], w_conv_proj[l], w_out[l])
        for gi in range(N_GROUPS):
            kv_p[gi].append(kvp_l[gi])
            kv_s[gi].append(kvs_l[gi])
        conv_p.append(cp_l)
        conv_s.append(cs_l)
    stack = lambda xs: xs[0][None] if len(xs) == 1 else jnp.stack(xs, axis=0)
    return (yp, ys, stack(kv_p[0]), stack(kv_p[1]), stack(kv_p[2]), stack(conv_p),
            stack(kv_s[0]), stack(kv_s[1]), stack(kv_s[2]), stack(conv_s))
```

```python
import functools

import jax
import jax.numpy as jnp
from jax import lax
from jax.experimental import pallas as pl
from jax.experimental.pallas import tpu as pltpu

F32 = jnp.float32
BF16 = jnp.bfloat16

HEAD_DIM = 128
N_HEADS = 8
GROUPS = ((128, 1), (512, 4), (2048, 16))
N_GROUPS = len(GROUPS)
BAND = 128
D_ATT = N_HEADS * HEAD_DIM
ROT_DIM = HEAD_DIM // 4
ROT_HALF = ROT_DIM // 2
ROPE_THETA = 500000.0
EPS = 1e-6
SCALE = HEAD_DIM ** -0.5
NEG_INF = -1e30
PAST_LEN = 16384
CONV_WIDTH = 3

V7X_VMEM_BYTES = 64 * 1024 * 1024
SUBLANES = 8

SUPER = BAND * max(d for _, d in GROUPS)
CHUNK = 512


def _vmem_limit(estimate_bytes):
    return int(min(V7X_VMEM_BYTES - 8 * 1024 * 1024, max(32 * 1024 * 1024, estimate_bytes)))


ROLL_ROWS = 300
ROLL_BUFFERS = 4
ROLL_LAG = 2


def _roll_plan(caches, shift):
    kinds, first = [], 0
    for g, c in enumerate(caches):
        db, window = c.shape[:2]
        keep = window - shift
        rows = max(r for r in range(1, ROLL_ROWS + 1) if keep % r == 0)
        kinds.append((g, rows, keep // rows, first, db * (keep // rows)))
        first += kinds[-1][4]
    return tuple(kinds), first


def _roll_step(s, kinds, shift, caches, news, outs, buf, new_buf, sem_in, sem_out, sem_new):
    def copy(kind, t, inbound):
        g, rows, per_batch, first, _ = kind
        b = (t - first) // per_batch
        c = (t - first) % per_batch
        slot = t % ROLL_BUFFERS
        stage = buf.at[slot, pl.ds(0, rows)]
        if inbound:
            return pltpu.make_async_copy(caches[g].at[b, pl.ds(shift + c * rows, rows)], stage, sem_in.at[slot])
        return pltpu.make_async_copy(stage, outs[g].at[b, pl.ds(c * rows, rows)], sem_out.at[slot])

    def for_item(t, fn):
        for kind in kinds:
            pl.when(jnp.logical_and(t >= kind[3], t < kind[3] + kind[4]))(functools.partial(fn, kind, t))

    def turn_around(kind, t):
        copy(kind, t, True).wait()
        copy(kind, t, False).start()

    for_item(s - ROLL_BUFFERS, lambda kind, t: copy(kind, t, False).wait())
    for_item(s, lambda kind, t: copy(kind, t, True).start())
    for_item(s - ROLL_LAG, turn_around)

    def new_copy(g, inbound):
        keep = outs[g].shape[1] - shift
        if inbound:
            return pltpu.make_async_copy(news[g], new_buf.at[g], sem_new.at[0, g])
        return pltpu.make_async_copy(new_buf.at[g], outs[g].at[:, pl.ds(keep, shift)], sem_new.at[1, g])

    @pl.when(s == 0)
    def _():
        for g in range(len(news)):
            new_copy(g, True).start()

    @pl.when(s == 1)
    def _():
        for g in range(len(news)):
            new_copy(g, True).wait()
            new_copy(g, False).start()

    @pl.when(s == 2)
    def _():
        for g in range(len(news)):
            new_copy(g, False).wait()


def _proj_body(*refs, chunk, roll):
    if roll is None:
        x_ref, g_ref, w_ref, o_ref, xn_ref = refs
    else:
        kinds, shift, n = roll
        x_ref, g_ref, w_ref = refs[:3]
        caches, news = refs[3:3 + n], refs[3 + n:3 + 2 * n]
        o_ref = refs[3 + 2 * n]
        outs = refs[4 + 2 * n:4 + 3 * n]
        xn_ref, buf, new_buf, sem_in, sem_out, sem_new = refs[4 + 3 * n:]
        s = pl.program_id(0) * pl.num_programs(1) + pl.program_id(1)
        _roll_step(s, kinds, shift, caches, news, outs, buf, new_buf, sem_in, sem_out, sem_new)
    tm = x_ref.shape[0]

    @pl.when(pl.program_id(1) == 0)
    def _normalise():
        def step(c, carry):
            r0 = pl.multiple_of(c * chunk, chunk)
            x = x_ref[pl.ds(r0, chunk), :]
            ms = jnp.mean(x * x, axis=-1, keepdims=True)
            xn_ref[pl.ds(r0, chunk), :] = ((x * lax.rsqrt(ms + EPS)) * g_ref[...]).astype(BF16)
            return carry
        lax.fori_loop(0, tm // chunk, step, 0)

    o_ref[...] = jnp.dot(xn_ref[...], w_ref[...], preferred_element_type=F32)


def _proj(x, g, w_bf16, *, tm, tn, roll=None):
    m, d = x.shape
    n = w_bf16.shape[1]
    assert m % tm == 0 and n % tn == 0
    chunk = min(tm, 128)
    est = 2 * tm * d * 4 + tm * d * 2 + 2 * d * tn * 2 + 2 * tm * tn * 4 + (4 << 20)
    grid = (m // tm, n // tn)
    in_specs = [
        pl.BlockSpec((tm, d), lambda i, j: (i, 0)),
        pl.BlockSpec((1, d), lambda i, j: (0, 0)),
        pl.BlockSpec((d, tn), lambda i, j: (0, j)),
    ]
    out_specs = [pl.BlockSpec((tm, tn), lambda i, j: (i, j))]
    out_shape = [jax.ShapeDtypeStruct((m, n), F32)]
    scratch = [pltpu.VMEM((tm, d), BF16)]
    args = [x, g, w_bf16]
    roll_static = None
    if roll is not None:
        caches, news = roll
        nc = len(caches)
        shift = news[0].shape[1]
        kinds, n_items = _roll_plan(caches, shift)
        assert n_items + ROLL_BUFFERS < grid[0] * grid[1] and all(nw.shape == news[0].shape for nw in news)
        roll_static = (kinds, shift, nc)
        any_spec = pl.BlockSpec(memory_space=pl.ANY)
        in_specs += [any_spec] * (2 * nc)
        out_specs += [any_spec] * nc
        out_shape += [jax.ShapeDtypeStruct(c.shape, c.dtype) for c in caches]
        row_shape = caches[0].shape[2:]
        max_rows = max(k[1] for k in kinds)
        scratch += [
            pltpu.VMEM((ROLL_BUFFERS, max_rows) + row_shape, F32),
            pltpu.VMEM((nc,) + news[0].shape, F32),
            pltpu.SemaphoreType.DMA((ROLL_BUFFERS,)),
            pltpu.SemaphoreType.DMA((ROLL_BUFFERS,)),
            pltpu.SemaphoreType.DMA((2, nc)),
        ]
        args += [*caches, *news]
        row_bytes = 4 * row_shape[0] * row_shape[1]
        est += (ROLL_BUFFERS * max_rows + nc * news[0].shape[0] * shift) * row_bytes
    out = pl.pallas_call(
        functools.partial(_proj_body, chunk=chunk, roll=roll_static),
        grid=grid,
        in_specs=in_specs,
        out_specs=out_specs,
        out_shape=out_shape,
        scratch_shapes=scratch,
        compiler_params=pltpu.CompilerParams(
            dimension_semantics=("arbitrary", "arbitrary"),
            vmem_limit_bytes=_vmem_limit(est)),
        name="proj" if roll is None else "proj_roll",
    )(*args)
    return out[0] if roll is None else (out[0], out[1:])


def _rope_tables(pos):
    n = pos.shape[0]
    inv = ROPE_THETA ** (-jnp.arange(ROT_HALF, dtype=F32) * (2.0 / ROT_DIM))
    ang = pos[:, None] * inv[None, :]
    cos, sin = jnp.cos(ang), jnp.sin(ang)
    c = jnp.concatenate([cos, cos, jnp.ones((n, HEAD_DIM - ROT_DIM), F32)], axis=1)
    sa = jnp.concatenate([-sin, jnp.zeros((n, HEAD_DIM - ROT_HALF), F32)], axis=1)
    sb = jnp.concatenate([jnp.zeros((n, ROT_HALF), F32), sin, jnp.zeros((n, HEAD_DIM - ROT_DIM), F32)], axis=1)
    return jnp.concatenate([c, sa, sb], axis=1)


def _norm_rope(x, w, c, sa, sb):
    r = lax.rsqrt(jnp.mean(x * x, axis=-1, keepdims=True) + EPS)
    xn = (x * r) * w
    up = pltpu.roll(xn, HEAD_DIM - ROT_HALF, 1)
    dn = pltpu.roll(xn, ROT_HALF, 1)
    return xn * c + up * sa + dn * sb


def _group_weights(l0, l1, l2):
    mx = jnp.maximum(jnp.maximum(l0, l1), l2)
    e0, e1, e2 = jnp.exp(l0 - mx), jnp.exp(l1 - mx), jnp.exp(l2 - mx)
    den = (e0 + e1) + e2
    return e0 / den, e1 / den, e2 / den


def _attn_prompt_body(*refs):
    qkv_refs = refs[:3 * N_GROUPS]
    (rope_ref, qw_ref, kw_ref, o_ref, kv_ref, kbuf, vbuf, qbuf, obuf, lbuf) = refs[3 * N_GROUPS:]
    sb = pl.program_id(2)
    ch = pl.program_id(3)
    n_chunks = pl.num_programs(3)

    @pl.when(jnp.logical_and(sb == 0, ch == 0))
    def _no_previous_superblock():
        kbuf[:, 0:SUPER, :] = jnp.zeros((N_GROUPS, SUPER, HEAD_DIM), F32)
        vbuf[:, 0:SUPER, :] = jnp.zeros((N_GROUPS, SUPER, HEAD_DIM), F32)

    c = rope_ref[:, 0:HEAD_DIM]
    sa = rope_ref[:, HEAD_DIM:2 * HEAD_DIM]
    sbn = rope_ref[:, 2 * HEAD_DIM:3 * HEAD_DIM]
    r0 = pl.multiple_of(ch * CHUNK, CHUNK)
    for g in range(N_GROUPS):
        q_ref, k_ref, v_ref = qkv_refs[3 * g:3 * g + 3]
        qn = _norm_rope(q_ref[...], qw_ref[g:g + 1, :], c, sa, sbn)
        kn = _norm_rope(k_ref[...], kw_ref[g:g + 1, :], c, sa, sbn)
        v = v_ref[...]
        qbuf[g, pl.ds(r0, CHUNK), :] = qn
        kbuf[g, pl.ds(SUPER + r0, CHUNK), :] = kn
        vbuf[g, pl.ds(SUPER + r0, CHUNK), :] = v
        kv_ref[g, 0] = kn
        kv_ref[g, 1] = v

    @pl.when(ch == n_chunks - 1)
    def _attend():
        qi = lax.broadcasted_iota(jnp.int32, (BAND, 2 * BAND), 0)
        kj = lax.broadcasted_iota(jnp.int32, (BAND, 2 * BAND), 1)
        hi = qi + BAND
        lo_first = jnp.maximum(qi, jnp.where(sb == 0, BAND, 0))
        nt = (((1,), (1,)), ((), ()))
        for g, (_, dil) in enumerate(GROUPS):
            for res in range(dil):
                for blk in range(SUPER // (BAND * dil)):
                    start = res + dil * BAND * blk
                    q_rows = pl.ds(start, BAND, stride=dil) if dil > 1 else pl.ds(start, BAND)
                    k_start = SUPER + start - dil * BAND
                    k_rows = pl.ds(k_start, 2 * BAND, stride=dil) if dil > 1 else pl.ds(k_start, 2 * BAND)
                    q = qbuf[g, q_rows, :].astype(BF16)
                    kc = kbuf[g, k_rows, :].astype(BF16)
                    vc = vbuf[g, k_rows, :].astype(BF16)
                    s = lax.dot_general(q, kc, nt, preferred_element_type=F32) * SCALE
                    lo = lo_first if blk == 0 else qi
                    s = jnp.where(kj >= lo, jnp.where(kj <= hi, s, NEG_INF), NEG_INF)
                    m = jnp.max(s, axis=-1, keepdims=True)
                    p = jnp.exp(s - m)
                    l = jnp.sum(p, axis=-1, keepdims=True)
                    o = jnp.dot((p / l).astype(BF16), vc, preferred_element_type=F32)
                    obuf[g, q_rows, :] = o
                    lbuf[g, q_rows, :] = jnp.broadcast_to(m + jnp.log(l), (BAND, HEAD_DIM))

        rows_per_step = 256

        def combine(i, carry):
            rows = pl.ds(pl.multiple_of(i * rows_per_step, rows_per_step), rows_per_step)
            w0, w1, w2 = _group_weights(lbuf[0, rows, :], lbuf[1, rows, :], lbuf[2, rows, :])
            o_ref[rows, :] = (w0 * obuf[0, rows, :] + w1 * obuf[1, rows, :]) + w2 * obuf[2, rows, :]
            cur = pl.ds(pl.multiple_of(SUPER + i * rows_per_step, rows_per_step), rows_per_step)
            for g in range(N_GROUPS):
                kbuf[g, rows, :] = kbuf[g, cur, :]
                vbuf[g, rows, :] = vbuf[g, cur, :]
            return carry
        lax.fori_loop(0, SUPER // rows_per_step, combine, 0)


def _attn_prompt(z, rope, qw, kw, batch, seq):
    assert seq % SUPER == 0 and SUPER % CHUNK == 0
    nsb = seq // SUPER
    nch = SUPER // CHUNK

    def zspec(g, which):
        return pl.BlockSpec(
            (CHUNK, HEAD_DIM),
            lambda b, h, sb, ch: ((b * nsb + sb) * nch + ch, (3 * g + which) * N_HEADS + h))

    in_specs = [zspec(g, which) for g in range(N_GROUPS) for which in range(3)]
    in_specs += [
        pl.BlockSpec((CHUNK, 3 * HEAD_DIM), lambda b, h, sb, ch: (sb * nch + ch, 0)),
        pl.BlockSpec((N_GROUPS, HEAD_DIM), lambda b, h, sb, ch: (0, 0)),
        pl.BlockSpec((N_GROUPS, HEAD_DIM), lambda b, h, sb, ch: (0, 0)),
    ]
    est = (2 * (9 * CHUNK * HEAD_DIM + CHUNK * 3 * HEAD_DIM + SUPER * HEAD_DIM + 6 * CHUNK * HEAD_DIM) * 4
           + N_GROUPS * (2 * 2 * SUPER + 3 * SUPER) * HEAD_DIM * 4 + (12 << 20))
    return pl.pallas_call(
        _attn_prompt_body,
        grid=(batch, N_HEADS, nsb, nch),
        in_specs=in_specs,
        out_specs=[
            pl.BlockSpec((SUPER, HEAD_DIM), lambda b, h, sb, ch: (b * nsb + sb, h)),
            pl.BlockSpec((None, None, N_GROUPS, 2, CHUNK, HEAD_DIM),
                         lambda b, h, sb, ch: (b, h, 0, 0, jnp.where(sb == nsb - 1, ch, 0), 0)),
        ],
        out_shape=[
            jax.ShapeDtypeStruct((batch * seq, D_ATT), F32),
            jax.ShapeDtypeStruct((batch, N_HEADS, N_GROUPS, 2, SUPER, HEAD_DIM), F32),
        ],
        scratch_shapes=[
            pltpu.VMEM((N_GROUPS, 2 * SUPER, HEAD_DIM), F32),
            pltpu.VMEM((N_GROUPS, 2 * SUPER, HEAD_DIM), F32),
            pltpu.VMEM((N_GROUPS, SUPER, HEAD_DIM), F32),
            pltpu.VMEM((N_GROUPS, SUPER, HEAD_DIM), F32),
            pltpu.VMEM((N_GROUPS, SUPER, HEAD_DIM), F32),
        ],
        compiler_params=pltpu.CompilerParams(
            dimension_semantics=("arbitrary", "arbitrary", "arbitrary", "arbitrary"),
            vmem_limit_bytes=_vmem_limit(est)),
        name="attn_prompt",
    )(*([z] * (3 * N_GROUPS)), rope, qw, kw)


def _attn_sample_body(qkv_ref, c0_ref, c1_ref, c2_ref, rope_ref, qw_ref, kw_ref,
                      o_ref, n0_ref, n1_ref, n2_ref):
    t_new = qkv_ref.shape[0]
    cache_refs = (c0_ref, c1_ref, c2_ref)
    new_refs = (n0_ref, n1_ref, n2_ref)
    outs = [[None] * N_GROUPS for _ in range(t_new)]
    lses = [[None] * N_GROUPS for _ in range(t_new)]
    for g, (_, dil) in enumerate(GROUPS):
        qs, ks, vs = [], [], []
        for t in range(t_new):
            c = rope_ref[t:t + 1, 0:HEAD_DIM]
            sa = rope_ref[t:t + 1, HEAD_DIM:2 * HEAD_DIM]
            sbn = rope_ref[t:t + 1, 2 * HEAD_DIM:3 * HEAD_DIM]
            qs.append(_norm_rope(qkv_ref[t, 3 * g], qw_ref[g:g + 1, :], c, sa, sbn))
            ks.append(_norm_rope(qkv_ref[t, 3 * g + 1], kw_ref[g:g + 1, :], c, sa, sbn))
            vs.append(qkv_ref[t, 3 * g + 2])
            new_refs[g][t, 0:N_HEADS, :] = ks[t]
            new_refs[g][t, N_HEADS:2 * N_HEADS, :] = vs[t]
        for t in range(t_new):
            res = t % dil
            first = -((res - t) // dil)
            kc = cache_refs[g][first:, res, 0:N_HEADS, :]
            vc = cache_refs[g][first:, res, N_HEADS:2 * N_HEADS, :]
            q = qs[t]
            s_c = jnp.sum(kc * q[None], axis=-1, keepdims=True) * SCALE
            new_keys = [tp for tp in range(t + 1) if (t - tp) % dil == 0]
            s_n = [jnp.sum(ks[tp] * q, axis=-1, keepdims=True) * SCALE for tp in new_keys]
            m = jnp.max(s_c, axis=0)
            for s in s_n:
                m = jnp.maximum(m, s)
            p_c = jnp.exp(s_c - m[None])
            p_n = [jnp.exp(s - m) for s in s_n]
            l = jnp.sum(p_c, axis=0)
            for p in p_n:
                l = l + p
            o = jnp.sum((p_c / l[None]) * vc, axis=0)
            for p, tp in zip(p_n, new_keys):
                o = o + (p / l) * vs[tp]
            outs[t][g] = o
            lses[t][g] = m + jnp.log(l)
    for t in range(t_new):
        w0, w1, w2 = _group_weights(*lses[t])
        o_ref[t] = (w0 * outs[t][0] + w1 * outs[t][1]) + w2 * outs[t][2]


def _attn_sample(qkv, caches, rope, qw, kw):
    db, t_new = qkv.shape[:2]
    views, cache_specs, cache_bytes = [], [], 0
    for (window, dil), cache in zip(GROUPS, caches):
        assert cache.shape[1] == window == BAND * dil and t_new <= BAND
        n_res = min(dil, t_new)
        views.append(cache.reshape(db, BAND, dil, 2 * N_HEADS, HEAD_DIM))
        cache_specs.append(pl.BlockSpec((None, BAND, n_res, 2 * N_HEADS, HEAD_DIM), lambda b: (b, 0, 0, 0, 0)))
        cache_bytes += BAND * n_res * 2 * N_HEADS * HEAD_DIM * 4
    new_spec = pl.BlockSpec((None, t_new, 2 * N_HEADS, HEAD_DIM), lambda b: (b, 0, 0, 0))
    new_shape = jax.ShapeDtypeStruct((db, t_new, 2 * N_HEADS, HEAD_DIM), F32)
    return pl.pallas_call(
        _attn_sample_body,
        grid=(db,),
        in_specs=[
            pl.BlockSpec((None, t_new, 3 * N_GROUPS, N_HEADS, HEAD_DIM), lambda b: (b, 0, 0, 0, 0)),
            *cache_specs,
            pl.BlockSpec((t_new, 3 * HEAD_DIM), lambda b: (0, 0)),
            pl.BlockSpec((N_GROUPS, HEAD_DIM), lambda b: (0, 0)),
            pl.BlockSpec((N_GROUPS, HEAD_DIM), lambda b: (0, 0)),
        ],
        out_specs=[pl.BlockSpec((None, t_new, N_HEADS, HEAD_DIM), lambda b: (b, 0, 0, 0)),
                   new_spec, new_spec, new_spec],
        out_shape=[jax.ShapeDtypeStruct((db, t_new, N_HEADS, HEAD_DIM), F32),
                   new_shape, new_shape, new_shape],
        compiler_params=pltpu.CompilerParams(
            dimension_semantics=("arbitrary",), vmem_limit_bytes=_vmem_limit(2 * cache_bytes + (16 << 20))),
        name="attn_sample",
    )(qkv, *views, rope, qw, kw)


def _sigmoid(x):
    return 1.0 / (1.0 + jnp.exp(-x))


def _merge_body(*refs, sample, rows_per_seq):
    if sample:
        (oa_ref, ag_ref, hc_ref, bc_ref, cc_ref, cg_ref, mla_ref, mlc_ref, x_ref, cw_ref,
         wa_ref, wc_ref, wo_ref, p1_ref, p2_ref, y_ref, u_ref, ubuf_ref) = refs
    else:
        (oa_ref, ag_ref, hc_ref, bc_ref, cc_ref, cg_ref, mla_ref, mlc_ref, x_ref, cw_ref,
         wa_ref, wc_ref, wo_ref, y_ref, u_ref, ubuf_ref) = refs
    tm = x_ref.shape[0]

    ag = ag_ref[...]
    ain = (oa_ref[...] * (ag * _sigmoid(ag))).astype(BF16)
    a = jnp.dot(ain, wa_ref[...], preferred_element_type=F32)

    u = cc_ref[...] * hc_ref[...]
    if sample:
        ubuf_ref[0:SUBLANES, :] = jnp.zeros((SUBLANES, u.shape[1]), F32)
    else:
        @pl.when(pl.program_id(0) % (rows_per_seq // tm) == 0)
        def _sequence_start():
            ubuf_ref[0:SUBLANES, :] = jnp.zeros((SUBLANES, u.shape[1]), F32)
    ubuf_ref[SUBLANES:SUBLANES + tm, :] = u
    u1 = ubuf_ref[SUBLANES - 1:SUBLANES - 1 + tm, :]
    u2 = ubuf_ref[SUBLANES - 2:SUBLANES - 2 + tm, :]
    if sample:
        t = lax.broadcasted_iota(jnp.int32, (tm, 1), 0) % rows_per_seq
        u1 = jnp.where(t < 1, p1_ref[...], u1)
        u2 = jnp.where(t < 2, p2_ref[...], u2)
        u_ref[...] = u
    else:
        tail = ubuf_ref[tm:tm + SUBLANES, :]
        ubuf_ref[0:SUBLANES, :] = tail
        u_ref[...] = tail
    conv = (cw_ref[0:1, :] * u2 + cw_ref[1:2, :] * u1) + cw_ref[2:3, :] * u
    cg = cg_ref[...]
    cin = ((bc_ref[...] * conv) * (cg * _sigmoid(cg))).astype(BF16)
    cbr = jnp.dot(cin, wc_ref[...], preferred_element_type=F32)

    m = _sigmoid(mla_ref[...]) * a + _sigmoid(mlc_ref[...]) * cbr
    y_ref[...] = x_ref[...] + jnp.dot(m.astype(BF16), wo_ref[...], preferred_element_type=F32)


def _merge(x, z, o_att, conv_w, wa, wc, wo, *, tm, rows_per_seq, state=None):
    m_rows, d = x.shape
    n = z.shape[1]
    sample = state is not None
    assert m_rows % tm == 0 and (sample or rows_per_seq % tm == 0)
    assert n == 3 * N_GROUPS * D_ATT + D_ATT + 6 * d and d % D_ATT == 0
    zb = (3 * N_GROUPS * D_ATT + D_ATT) // d
    row = lambda i: (i, 0)
    const = lambda i: (0, 0)
    zcol = lambda j: (lambda i: (i, j))
    resident = dict(pipeline_mode=pl.Buffered(1))
    in_specs = [
        pl.BlockSpec((tm, D_ATT), row),
        pl.BlockSpec((tm, D_ATT), zcol(3 * N_GROUPS)),
        pl.BlockSpec((tm, d), zcol(zb)),
        pl.BlockSpec((tm, d), zcol(zb + 1)),
        pl.BlockSpec((tm, d), zcol(zb + 2)),
        pl.BlockSpec((tm, d), zcol(zb + 3)),
        pl.BlockSpec((tm, d), zcol(zb + 4)),
        pl.BlockSpec((tm, d), zcol(zb + 5)),
        pl.BlockSpec((tm, d), row),
        pl.BlockSpec((CONV_WIDTH, d), const),
        pl.BlockSpec((D_ATT, d), const, **resident),
        pl.BlockSpec((d, d), const, **resident),
        pl.BlockSpec((d, d), const, **resident),
    ]
    args = [o_att, z, z, z, z, z, z, z, x, conv_w, wa, wc, wo]
    if sample:
        in_specs += [pl.BlockSpec((tm, d), row), pl.BlockSpec((tm, d), row)]
        args += list(state)
        u_spec = pl.BlockSpec((tm, d), row)
        u_shape = jax.ShapeDtypeStruct((m_rows, d), F32)
    else:
        per_seq = rows_per_seq // tm
        u_spec = pl.BlockSpec((None, SUBLANES, d), lambda i: (i // per_seq, 0, 0))
        u_shape = jax.ShapeDtypeStruct((m_rows // rows_per_seq, SUBLANES, d), F32)
    est = (2 * tm * (2 * D_ATT + 8 * d) * 4 + (D_ATT * d + 2 * d * d) * 2
           + 10 * tm * d * 4 + (4 << 20))
    return pl.pallas_call(
        functools.partial(_merge_body, sample=sample, rows_per_seq=rows_per_seq),
        grid=(m_rows // tm,),
        in_specs=in_specs,
        out_specs=[pl.BlockSpec((tm, d), row), u_spec],
        out_shape=[jax.ShapeDtypeStruct((m_rows, d), F32), u_shape],
        scratch_shapes=[pltpu.VMEM((tm + 2 * SUBLANES, d), F32)],
        compiler_params=pltpu.CompilerParams(
            dimension_semantics=("arbitrary",), vmem_limit_bytes=_vmem_limit(est)),
        name="merge_sample" if sample else "merge_prompt",
    )(*args)


def _layer(xp, xs, caches, state, norm_w, w_in, q_norm_w, k_norm_w, conv_w, w_att_proj, w_conv_proj, w_out):
    b, s, d = xp.shape
    db, t_new, _ = xs.shape
    w_in_b = w_in.astype(BF16)
    wa, wc, wo = w_att_proj.astype(BF16), w_conv_proj.astype(BF16), w_out.astype(BF16)
    g = norm_w.reshape(1, d)

    rows_s = db * t_new
    zs = _proj(xs.reshape(rows_s, d), g, w_in_b, tm=rows_s, tn=1024)
    rope_s = _rope_tables(PAST_LEN + jnp.arange(t_new, dtype=F32))
    qkv_s = zs[:, :3 * N_GROUPS * D_ATT].reshape(db, t_new, 3 * N_GROUPS, N_HEADS, HEAD_DIM)
    flat_caches = [c.reshape(db, c.shape[1], 2 * N_HEADS, HEAD_DIM) for c in caches]
    o_att_s, *kv_new = _attn_sample(qkv_s, flat_caches, rope_s, q_norm_w, k_norm_w)
    p1 = jnp.pad(state[:, 1:2], ((0, 0), (0, t_new - 1), (0, 0))).reshape(rows_s, d)
    p2 = jnp.pad(state, ((0, 0), (0, t_new - 2), (0, 0))).reshape(rows_s, d)
    ys, u_s = _merge(xs.reshape(rows_s, d), zs, o_att_s.reshape(rows_s, D_ATT), conv_w, wa, wc, wo,
                     tm=rows_s, rows_per_seq=t_new, state=(p1, p2))
    conv_s = u_s.reshape(db, t_new, d)[:, t_new - (CONV_WIDTH - 1):]

    zp, rolled = _proj(xp.reshape(b * s, d), g, w_in_b, tm=1024, tn=1024, roll=(flat_caches, kv_new))
    kv_s = [c.reshape(db, -1, 2, N_HEADS, HEAD_DIM) for c in rolled]
    rope_p = _rope_tables(jnp.arange(s, dtype=F32))
    o_att_p, kv_tail = _attn_prompt(zp, rope_p, q_norm_w, k_norm_w, b, s)
    kv_p = []
    for gi, (window, dil) in enumerate(GROUPS):
        assert window == BAND * dil and window <= SUPER
        tail = kv_tail[:, :, gi, :, SUPER - window:, :]
        kv_p.append(jnp.transpose(tail, (0, 3, 2, 1, 4)))
    yp, u_tail = _merge(xp.reshape(b * s, d), zp, o_att_p, conv_w, wa, wc, wo, tm=128, rows_per_seq=s)
    conv_p = u_tail[:, SUBLANES - (CONV_WIDTH - 1):]

    return yp.reshape(b, s, d), ys.reshape(db, t_new, d), kv_p, conv_p, kv_s, conv_s


def kernel(x_prompt, x_sample, cache_kv_w128, cache_kv_w512, cache_kv_w2048, state_conv, norm_w, w_in, q_norm_w,
           k_norm_w, conv_w, w_att_proj, w_conv_proj, w_out):
    depth = norm_w.shape[0]
    caches = (cache_kv_w128, cache_kv_w512, cache_kv_w2048)
    yp, ys = x_prompt, x_sample
    kv_p = [[] for _ in GROUPS]
    kv_s = [[] for _ in GROUPS]
    conv_p, conv_s = [], []
    for l in range(depth):
        yp, ys, kvp_l, cp_l, kvs_l, cs_l = _layer(
            yp, ys, [c[l] for c in caches], state_conv[l], norm_w[l], w_in[l], q_norm_w[l], k_norm_w[l],
            conv_w[l], w_att_proj[l], w_conv_proj[l], w_out[l])
        for gi in range(N_GROUPS):
            kv_p[gi].append(kvp_l[gi])
            kv_s[gi].append(kvs_l[gi])
        conv_p.append(cp_l)
        conv_s.append(cs_l)
    stack = lambda xs: xs[0][None] if len(xs) == 1 else jnp.stack(xs, axis=0)
    return (yp, ys, stack(kv_p[0]), stack(kv_p[1]), stack(kv_p[2]), stack(conv_p),
            stack(kv_s[0]), stack(kv_s[1]), stack(kv_s[2]), stack(conv_s))
```

```python
import functools

import jax
import jax.numpy as jnp
from jax import lax
from jax.experimental import pallas as pl
from jax.experimental.pallas import tpu as pltpu

F32 = jnp.float32
BF16 = jnp.bfloat16

HEAD_DIM = 128
N_HEADS = 8
GROUPS = ((128, 1), (512, 4), (2048, 16))
N_GROUPS = len(GROUPS)
N_QKV = 3 * N_GROUPS
BAND = 128
D_ATT = N_HEADS * HEAD_DIM
ROT_DIM = HEAD_DIM // 4
ROT_HALF = ROT_DIM // 2
ROPE_THETA = 500000.0
EPS = 1e-6
SCALE = HEAD_DIM ** -0.5
NEG_INF = -1e30
PAST_LEN = 16384
CONV_WIDTH = 3

V7X_VMEM_BYTES = 64 * 1024 * 1024
SUBLANES = 8

TILE = 1024
SUPER = BAND * max(d for _, d in GROUPS)
TILES_PER_SUPER = SUPER // TILE


def _vmem_limit(estimate_bytes):
    return int(min(V7X_VMEM_BYTES - 8 * 1024 * 1024, max(32 * 1024 * 1024, estimate_bytes)))


def _rmsnorm_body(x_ref, g_ref, o_ref):
    x = x_ref[...]
    ms = jnp.mean(x * x, axis=-1, keepdims=True)
    o_ref[...] = ((x * lax.rsqrt(ms + EPS)) * g_ref[...]).astype(BF16)


def _rmsnorm(x, g, tm):
    m, d = x.shape
    assert m % tm == 0
    return pl.pallas_call(
        _rmsnorm_body,
        grid=(m // tm,),
        in_specs=[pl.BlockSpec((tm, d), lambda i: (i, 0)), pl.BlockSpec((1, d), lambda i: (0, 0))],
        out_specs=pl.BlockSpec((tm, d), lambda i: (i, 0)),
        out_shape=jax.ShapeDtypeStruct((m, d), BF16),
        compiler_params=pltpu.CompilerParams(dimension_semantics=("arbitrary",)),
        name="rmsnorm",
    )(x, g)


def _proj_plain_body(x_ref, w_ref, o_ref):
    o_ref[...] = jnp.dot(x_ref[...], w_ref[...], preferred_element_type=F32)


def _proj_plain(xn, w_bf16, tn):
    m, d = xn.shape
    n = w_bf16.shape[1]
    assert n % tn == 0
    return pl.pallas_call(
        _proj_plain_body,
        grid=(n // tn,),
        in_specs=[pl.BlockSpec((m, d), lambda j: (0, 0)), pl.BlockSpec((d, tn), lambda j: (0, j))],
        out_specs=pl.BlockSpec((m, tn), lambda j: (0, j)),
        out_shape=jax.ShapeDtypeStruct((m, n), F32),
        compiler_params=pltpu.CompilerParams(dimension_semantics=("arbitrary",)),
        name="proj_plain",
    )(xn, w_bf16)


def _rope_tables(pos):
    n = pos.shape[0]
    inv = ROPE_THETA ** (-jnp.arange(ROT_HALF, dtype=F32) * (2.0 / ROT_DIM))
    ang = pos[:, None] * inv[None, :]
    cos, sin = jnp.cos(ang), jnp.sin(ang)
    c = jnp.concatenate([cos, cos, jnp.ones((n, HEAD_DIM - ROT_DIM), F32)], axis=1)
    sa = jnp.concatenate([-sin, jnp.zeros((n, HEAD_DIM - ROT_HALF), F32)], axis=1)
    sb = jnp.concatenate([jnp.zeros((n, ROT_HALF), F32), sin, jnp.zeros((n, HEAD_DIM - ROT_DIM), F32)], axis=1)
    return jnp.stack([c, sa, sb], axis=0)


def _norm_rope(x, w, c, sa, sb):
    r = lax.rsqrt(jnp.mean(x * x, axis=-1, keepdims=True) + EPS)
    xn = (x * r) * w
    up = pltpu.roll(xn, HEAD_DIM - ROT_HALF, 1)
    dn = pltpu.roll(xn, ROT_HALF, 1)
    return xn * c + up * sa + dn * sb


def _group_weights(l0, l1, l2):
    mx = jnp.maximum(jnp.maximum(l0, l1), l2)
    e0, e1, e2 = jnp.exp(l0 - mx), jnp.exp(l1 - mx), jnp.exp(l2 - mx)
    den = (e0 + e1) + e2
    return e0 / den, e1 / den, e2 / den


ROLL_ROWS = 300
ROLL_BUFFERS = 4
ROLL_LAG = 2


def _roll_plan(caches, shift):
    kinds, first = [], 0
    for g, c in enumerate(caches):
        db, window = c.shape[:2]
        keep = window - shift
        rows = max(r for r in range(1, ROLL_ROWS + 1) if keep % r == 0)
        kinds.append((g, rows, keep // rows, first, db * (keep // rows)))
        first += kinds[-1][4]
    return tuple(kinds), first


def _roll_step(s, kinds, shift, caches, news, outs, buf, new_buf, sem_in, sem_out, sem_new):
    def copy(kind, t, inbound):
        g, rows, per_batch, first, _ = kind
        b = (t - first) // per_batch
        c = (t - first) % per_batch
        slot = t % ROLL_BUFFERS
        stage = buf.at[slot, pl.ds(0, rows)]
        if inbound:
            return pltpu.make_async_copy(caches[g].at[b, pl.ds(shift + c * rows, rows)], stage, sem_in.at[slot])
        return pltpu.make_async_copy(stage, outs[g].at[b, pl.ds(c * rows, rows)], sem_out.at[slot])

    def for_item(t, fn):
        for kind in kinds:
            pl.when(jnp.logical_and(t >= kind[3], t < kind[3] + kind[4]))(functools.partial(fn, kind, t))

    def turn_around(kind, t):
        copy(kind, t, True).wait()
        copy(kind, t, False).start()

    for_item(s - ROLL_BUFFERS, lambda kind, t: copy(kind, t, False).wait())
    for_item(s, lambda kind, t: copy(kind, t, True).start())
    for_item(s - ROLL_LAG, turn_around)

    def new_copy(g, inbound):
        keep = outs[g].shape[1] - shift
        if inbound:
            return pltpu.make_async_copy(news[g], new_buf.at[g], sem_new.at[0, g])
        return pltpu.make_async_copy(new_buf.at[g], outs[g].at[:, pl.ds(keep, shift)], sem_new.at[1, g])

    @pl.when(s == 0)
    def _():
        for g in range(len(news)):
            new_copy(g, True).start()

    @pl.when(s == 1)
    def _():
        for g in range(len(news)):
            new_copy(g, True).wait()
            new_copy(g, False).start()

    @pl.when(s == 2)
    def _():
        for g in range(len(news)):
            new_copy(g, False).wait()


def _qkv_epilogue(jj, res, rope_ref, qw_ref, kw_ref, qkv_ref, tails, sem_tail, batch, tile_in_seq, tiles_per_seq):
    g, which = divmod(jj, 3)
    window, dil = GROUPS[g]
    per = TILE // dil
    w = None if which == 2 else (qw_ref if which == 0 else kw_ref)[g:g + 1, :]
    for r in range(dil):
        rows = pl.ds(r, per, stride=dil) if dil > 1 else pl.ds(0, per)
        dst = slice(r * per, (r + 1) * per)
        if which < 2:
            c, sa, sb = rope_ref[0, rows, :], rope_ref[1, rows, :], rope_ref[2, rows, :]
        for h in range(N_HEADS):
            cols = slice(h * HEAD_DIM, (h + 1) * HEAD_DIM)
            x = res[h, rows, :]
            qkv_ref[dst, cols] = (x if which == 2 else _norm_rope(x, w, c, sa, sb)).astype(BF16)
    if which == 0:
        return
    n_tail_tiles = -(-window // TILE)
    tail_rows = min(window, TILE)
    first_tail_tile = tiles_per_seq - n_tail_tiles

    @pl.when(tile_in_seq >= first_tail_tile)
    def _tail():
        lo = TILE - tail_rows
        if which == 1:
            c, sa, sb = rope_ref[0, lo:TILE, :], rope_ref[1, lo:TILE, :], rope_ref[2, lo:TILE, :]
            for h in range(N_HEADS):
                res[h, lo:TILE, :] = _norm_rope(res[h, lo:TILE, :], w, c, sa, sb)
        dst_row = (tile_in_seq - first_tail_tile) * tail_rows
        cp = pltpu.make_async_copy(res.at[:, pl.ds(lo, tail_rows)],
                                   tails[g].at[batch, which - 1, :, pl.ds(dst_row, tail_rows)], sem_tail.at[0])
        cp.start()
        cp.wait()


def _proj_prompt_body(*refs, roll, tiles_per_seq, n_rest):
    kinds, shift, nc = roll
    xn_ref, w_ref, rope_ref, qw_ref, kw_ref = refs[:5]
    caches, news = refs[5:5 + nc], refs[5 + nc:5 + 2 * nc]
    qkv_ref, rest_ref = refs[5 + 2 * nc:7 + 2 * nc]
    tails = refs[7 + 2 * nc:7 + 2 * nc + N_GROUPS]
    rolled = refs[7 + 2 * nc + N_GROUPS:7 + 3 * nc + N_GROUPS]
    res_a, res_b, buf, new_buf, sem_in, sem_out, sem_new, sem_tail = refs[7 + 3 * nc + N_GROUPS:]
    i = pl.program_id(0)
    j = pl.program_id(1)
    _roll_step(i * pl.num_programs(1) + j, kinds, shift, caches, news, rolled, buf, new_buf, sem_in, sem_out, sem_new)
    batch = i // tiles_per_seq
    tile_in_seq = i % tiles_per_seq
    res = (res_a, res_b)

    def matmul():
        return jnp.dot(xn_ref[...], w_ref[...], preferred_element_type=F32)

    for jj in range(N_QKV + 1):
        def step(jj=jj):
            if jj < N_QKV:
                z = matmul()
                for h in range(N_HEADS):
                    res[jj % 2][h] = z[:, h * HEAD_DIM:(h + 1) * HEAD_DIM]
            else:
                rest_ref[...] = matmul()
            if jj >= 1:
                _qkv_epilogue(jj - 1, res[(jj - 1) % 2], rope_ref, qw_ref, kw_ref, qkv_ref, tails, sem_tail,
                              batch, tile_in_seq, tiles_per_seq)
        pl.when(j == jj)(step)

    @pl.when(j > N_QKV)
    def _plain():
        rest_ref[...] = matmul()


def _proj_prompt(xn, w_bf16, rope, qw, kw, caches, news, batch, seq):
    m, d = xn.shape
    n = w_bf16.shape[1]
    assert m == batch * seq and seq % TILE == 0 and n % TILE == 0 and D_ATT == TILE
    tiles_per_seq = seq // TILE
    nj = n // TILE
    n_rest = nj - N_QKV
    grid = (m // TILE, nj)
    nc = len(caches)
    shift = news[0].shape[1]
    kinds, n_items = _roll_plan(caches, shift)
    assert n_items + ROLL_BUFFERS < grid[0] * grid[1] and all(nw.shape == news[0].shape for nw in news)
    assert all(w % TILE == 0 or TILE % w == 0 for w, _ in GROUPS)

    def w_col(i, j):
        return (0, jnp.where(j < N_QKV, j, jnp.where(j < nj - 1, j + 1, N_QKV)))

    any_spec = pl.BlockSpec(memory_space=pl.ANY)
    in_specs = [
        pl.BlockSpec((TILE, d), lambda i, j: (i, 0)),
        pl.BlockSpec((d, TILE), w_col),
        pl.BlockSpec((3, TILE, HEAD_DIM), lambda i, j: (0, i % tiles_per_seq, 0)),
        pl.BlockSpec((N_GROUPS, HEAD_DIM), lambda i, j: (0, 0)),
        pl.BlockSpec((N_GROUPS, HEAD_DIM), lambda i, j: (0, 0)),
        *([any_spec] * (2 * nc)),
    ]
    out_specs = [
        pl.BlockSpec((None, None, TILE, TILE), lambda i, j: (jnp.clip(j - 1, 0, N_QKV - 1), i, 0, 0)),
        pl.BlockSpec((TILE, TILE), lambda i, j: (i, jnp.clip(j - N_QKV, 0, n_rest - 1))),
        *([any_spec] * (N_GROUPS + nc)),
    ]
    out_shape = [
        jax.ShapeDtypeStruct((N_QKV, m // TILE, TILE, TILE), BF16),
        jax.ShapeDtypeStruct((m, n_rest * TILE), F32),
        *[jax.ShapeDtypeStruct((batch, 2, N_HEADS, w, HEAD_DIM), F32) for w, _ in GROUPS],
        *[jax.ShapeDtypeStruct(c.shape, c.dtype) for c in caches],
    ]
    row_shape = caches[0].shape[2:]
    max_rows = max(k[1] for k in kinds)
    scratch = [
        pltpu.VMEM((N_HEADS, TILE, HEAD_DIM), F32),
        pltpu.VMEM((N_HEADS, TILE, HEAD_DIM), F32),
        pltpu.VMEM((ROLL_BUFFERS, max_rows) + row_shape, F32),
        pltpu.VMEM((nc,) + news[0].shape, F32),
        pltpu.SemaphoreType.DMA((ROLL_BUFFERS,)),
        pltpu.SemaphoreType.DMA((ROLL_BUFFERS,)),
        pltpu.SemaphoreType.DMA((2, nc)),
        pltpu.SemaphoreType.DMA((1,)),
    ]
    row_bytes = 4 * row_shape[0] * row_shape[1]
    est = (2 * TILE * d * 2 + 2 * d * TILE * 2 + 2 * TILE * 3 * HEAD_DIM * 4 + 2 * TILE * TILE * 2 + 2 * TILE * TILE * 4
           + 2 * TILE * TILE * 4 + (ROLL_BUFFERS * max_rows + nc * news[0].shape[0] * shift) * row_bytes + (6 << 20))
    out = pl.pallas_call(
        functools.partial(_proj_prompt_body, roll=(kinds, shift, nc), tiles_per_seq=tiles_per_seq, n_rest=n_rest),
        grid=grid,
        in_specs=in_specs,
        out_specs=out_specs,
        out_shape=out_shape,
        scratch_shapes=scratch,
        compiler_params=pltpu.CompilerParams(
            dimension_semantics=("arbitrary", "arbitrary"),
            vmem_limit_bytes=_vmem_limit(est)),
        name="proj_prompt",
    )(xn, w_bf16, rope, qw, kw, *caches, *news)
    return out[0], out[1], out[2:2 + N_GROUPS], out[2 + N_GROUPS:]


def _subsequence_rows(cur_ref, prev_ref, dil, res, l0, n):
    per = TILE // dil
    pieces = []
    l = l0
    while l < l0 + n:
        ll = l + TILES_PER_SUPER * per if l < 0 else l
        t, off = divmod(ll, per)
        stop = 0 if l < 0 else l0 + n
        take = min(per - off, stop - l)
        if l < 0:
            ref, t = prev_ref, t - (TILES_PER_SUPER - prev_ref.shape[0])
        else:
            ref = cur_ref
        pieces.append(ref[t, res * per + off:res * per + off + take, :])
        l += take
    return pieces[0] if len(pieces) == 1 else jnp.concatenate(pieces, axis=0)


def _attn_prompt_body(*refs):
    cur = refs[:N_QKV]
    prev = refs[N_QKV:N_QKV + 2 * N_GROUPS]
    o_ref, obuf, lbuf = refs[N_QKV + 2 * N_GROUPS:]
    sb = pl.program_id(2)
    qi = lax.broadcasted_iota(jnp.int32, (BAND, 2 * BAND), 0)
    kj = lax.broadcasted_iota(jnp.int32, (BAND, 2 * BAND), 1)
    hi = qi + BAND
    lo_first = jnp.maximum(qi, jnp.where(sb == 0, BAND, 0))
    nt = (((1,), (1,)), ((), ()))
    for g, (_, dil) in enumerate(GROUPS):
        q_ref, k_ref, v_ref = cur[3 * g:3 * g + 3]
        kp_ref, vp_ref = prev[2 * g:2 * g + 2]
        for res in range(dil):
            for blk in range(SUPER // (BAND * dil)):
                l0 = BAND * blk
                q = _subsequence_rows(q_ref, None, dil, res, l0, BAND)
                kc = _subsequence_rows(k_ref, kp_ref, dil, res, l0 - BAND, 2 * BAND)
                vc = _subsequence_rows(v_ref, vp_ref, dil, res, l0 - BAND, 2 * BAND)
                s = lax.dot_general(q, kc, nt, preferred_element_type=F32) * SCALE
                lo = lo_first if blk == 0 else qi
                s = jnp.where(kj >= lo, jnp.where(kj <= hi, s, NEG_INF), NEG_INF)
                m = jnp.max(s, axis=-1, keepdims=True)
                p = jnp.exp(s - m)
                l = jnp.sum(p, axis=-1, keepdims=True)
                o = jnp.dot((p / l).astype(BF16), vc, preferred_element_type=F32)
                start = dil * l0 + res
                rows = pl.ds(start, BAND, stride=dil) if dil > 1 else pl.ds(start, BAND)
                obuf[g, rows, :] = o
                lbuf[g, rows, :] = jnp.broadcast_to(m + jnp.log(l), (BAND, HEAD_DIM))

    rows_per_step = 256

    def combine(i, carry):
        rows = pl.ds(pl.multiple_of(i * rows_per_step, rows_per_step), rows_per_step)
        w0, w1, w2 = _group_weights(lbuf[0, rows, :], lbuf[1, rows, :], lbuf[2, rows, :])
        o_ref[rows, :] = (w0 * obuf[0, rows, :] + w1 * obuf[1, rows, :]) + w2 * obuf[2, rows, :]
        return carry
    lax.fori_loop(0, SUPER // rows_per_step, combine, 0)


def _attn_prompt(qkv, batch, seq):
    assert seq % SUPER == 0 and SUPER % TILE == 0
    nsb = seq // SUPER
    tiles_per_seq = seq // TILE
    in_specs = [
        pl.BlockSpec((None, TILES_PER_SUPER, TILE, HEAD_DIM), lambda b, h, sb, jj=jj: (jj, b * nsb + sb, 0, h))
        for jj in range(N_QKV)
    ]
    args = [qkv] * N_QKV
    prev_bytes = 0
    for g, (_, dil) in enumerate(GROUPS):
        n_prev = -(-BAND * dil // TILE)
        assert TILES_PER_SUPER % n_prev == 0
        per_super = TILES_PER_SUPER // n_prev
        for which in (1, 2):
            in_specs.append(pl.BlockSpec(
                (None, n_prev, TILE, HEAD_DIM),
                lambda b, h, sb, jj=3 * g + which, per_super=per_super:
                    (jj, jnp.maximum((b * nsb + sb) * per_super - 1, b * nsb * per_super), 0, h)))
            args.append(qkv)
            prev_bytes += n_prev * TILE * HEAD_DIM * 2
    est = (2 * (N_QKV * TILES_PER_SUPER * TILE * HEAD_DIM * 2 + prev_bytes + SUPER * HEAD_DIM * 4)
           + 2 * N_GROUPS * SUPER * HEAD_DIM * 4 + (12 << 20))
    return pl.pallas_call(
        _attn_prompt_body,
        grid=(batch, N_HEADS, nsb),
        in_specs=in_specs,
        out_specs=pl.BlockSpec((SUPER, HEAD_DIM), lambda b, h, sb: (b * nsb + sb, h)),
        out_shape=jax.ShapeDtypeStruct((batch * seq, D_ATT), F32),
        scratch_shapes=[
            pltpu.VMEM((N_GROUPS, SUPER, HEAD_DIM), F32),
            pltpu.VMEM((N_GROUPS, SUPER, HEAD_DIM), F32),
        ],
        compiler_params=pltpu.CompilerParams(
            dimension_semantics=("arbitrary", "arbitrary", "arbitrary"),
            vmem_limit_bytes=_vmem_limit(est)),
        name="attn_prompt",
    )(*args)


def _attn_sample_body(qkv_ref, c0_ref, c1_ref, c2_ref, rope_ref, qw_ref, kw_ref,
                      o_ref, n0_ref, n1_ref, n2_ref):
    t_new = qkv_ref.shape[0]
    cache_refs = (c0_ref, c1_ref, c2_ref)
    new_refs = (n0_ref, n1_ref, n2_ref)
    outs = [[None] * N_GROUPS for _ in range(t_new)]
    lses = [[None] * N_GROUPS for _ in range(t_new)]
    for g, (_, dil) in enumerate(GROUPS):
        qs, ks, vs = [], [], []
        for t in range(t_new):
            c, sa, sbn = rope_ref[0, t:t + 1, :], rope_ref[1, t:t + 1, :], rope_ref[2, t:t + 1, :]
            qs.append(_norm_rope(qkv_ref[t, 3 * g], qw_ref[g:g + 1, :], c, sa, sbn))
            ks.append(_norm_rope(qkv_ref[t, 3 * g + 1], kw_ref[g:g + 1, :], c, sa, sbn))
            vs.append(qkv_ref[t, 3 * g + 2])
            new_refs[g][t, 0:N_HEADS, :] = ks[t]
            new_refs[g][t, N_HEADS:2 * N_HEADS, :] = vs[t]
        for t in range(t_new):
            res = t % dil
            first = -((res - t) // dil)
            kc = cache_refs[g][first:, res, 0:N_HEADS, :]
            vc = cache_refs[g][first:, res, N_HEADS:2 * N_HEADS, :]
            q = qs[t]
            s_c = jnp.sum(kc * q[None], axis=-1, keepdims=True) * SCALE
            new_keys = [tp for tp in range(t + 1) if (t - tp) % dil == 0]
            s_n = [jnp.sum(ks[tp] * q, axis=-1, keepdims=True) * SCALE for tp in new_keys]
            m = jnp.max(s_c, axis=0)
            for s in s_n:
                m = jnp.maximum(m, s)
            p_c = jnp.exp(s_c - m[None])
            p_n = [jnp.exp(s - m) for s in s_n]
            l = jnp.sum(p_c, axis=0)
            for p in p_n:
                l = l + p
            o = jnp.sum((p_c / l[None]) * vc, axis=0)
            for p, tp in zip(p_n, new_keys):
                o = o + (p / l) * vs[tp]
            outs[t][g] = o
            lses[t][g] = m + jnp.log(l)
    for t in range(t_new):
        w0, w1, w2 = _group_weights(*lses[t])
        o_ref[t] = (w0 * outs[t][0] + w1 * outs[t][1]) + w2 * outs[t][2]


def _attn_sample(qkv, caches, rope, qw, kw):
    db, t_new = qkv.shape[:2]
    views, cache_specs, cache_bytes = [], [], 0
    for (window, dil), cache in zip(GROUPS, caches):
        assert cache.shape[1] == window == BAND * dil and t_new <= BAND
        n_res = min(dil, t_new)
        views.append(cache.reshape(db, BAND, dil, 2 * N_HEADS, HEAD_DIM))
        cache_specs.append(pl.BlockSpec((None, BAND, n_res, 2 * N_HEADS, HEAD_DIM), lambda b: (b, 0, 0, 0, 0)))
        cache_bytes += BAND * n_res * 2 * N_HEADS * HEAD_DIM * 4
    new_spec = pl.BlockSpec((None, t_new, 2 * N_HEADS, HEAD_DIM), lambda b: (b, 0, 0, 0))
    new_shape = jax.ShapeDtypeStruct((db, t_new, 2 * N_HEADS, HEAD_DIM), F32)
    return pl.pallas_call(
        _attn_sample_body,
        grid=(db,),
        in_specs=[
            pl.BlockSpec((None, t_new, 3 * N_GROUPS, N_HEADS, HEAD_DIM), lambda b: (b, 0, 0, 0, 0)),
            *cache_specs,
            pl.BlockSpec((3, t_new, HEAD_DIM), lambda b: (0, 0, 0)),
            pl.BlockSpec((N_GROUPS, HEAD_DIM), lambda b: (0, 0)),
            pl.BlockSpec((N_GROUPS, HEAD_DIM), lambda b: (0, 0)),
        ],
        out_specs=[pl.BlockSpec((None, t_new, N_HEADS, HEAD_DIM), lambda b: (b, 0, 0, 0)),
                   new_spec, new_spec, new_spec],
        out_shape=[jax.ShapeDtypeStruct((db, t_new, N_HEADS, HEAD_DIM), F32),
                   new_shape, new_shape, new_shape],
        compiler_params=pltpu.CompilerParams(
            dimension_semantics=("arbitrary",), vmem_limit_bytes=_vmem_limit(2 * cache_bytes + (16 << 20))),
        name="attn_sample",
    )(qkv, *views, rope, qw, kw)


def _sigmoid(x):
    return 1.0 / (1.0 + jnp.exp(-x))


def _merge_body(*refs, sample, rows_per_seq):
    if sample:
        (oa_ref, ag_ref, hc_ref, bc_ref, cc_ref, cg_ref, mla_ref, mlc_ref, x_ref, cw_ref,
         wa_ref, wc_ref, wo_ref, p1_ref, p2_ref, y_ref, u_ref, ubuf_ref) = refs
    else:
        (oa_ref, ag_ref, hc_ref, bc_ref, cc_ref, cg_ref, mla_ref, mlc_ref, x_ref, cw_ref,
         wa_ref, wc_ref, wo_ref, y_ref, u_ref, ubuf_ref) = refs
    tm = x_ref.shape[0]

    ag = ag_ref[...]
    ain = (oa_ref[...] * (ag * _sigmoid(ag))).astype(BF16)
    a = jnp.dot(ain, wa_ref[...], preferred_element_type=F32)

    u = cc_ref[...] * hc_ref[...]
    if sample:
        ubuf_ref[0:SUBLANES, :] = jnp.zeros((SUBLANES, u.shape[1]), F32)
    else:
        @pl.when(pl.program_id(0) % (rows_per_seq // tm) == 0)
        def _sequence_start():
            ubuf_ref[0:SUBLANES, :] = jnp.zeros((SUBLANES, u.shape[1]), F32)
    ubuf_ref[SUBLANES:SUBLANES + tm, :] = u
    u1 = ubuf_ref[SUBLANES - 1:SUBLANES - 1 + tm, :]
    u2 = ubuf_ref[SUBLANES - 2:SUBLANES - 2 + tm, :]
    if sample:
        t = lax.broadcasted_iota(jnp.int32, (tm, 1), 0) % rows_per_seq
        u1 = jnp.where(t < 1, p1_ref[...], u1)
        u2 = jnp.where(t < 2, p2_ref[...], u2)
        u_ref[...] = u
    else:
        tail = ubuf_ref[tm:tm + SUBLANES, :]
        ubuf_ref[0:SUBLANES, :] = tail
        u_ref[...] = tail
    conv = (cw_ref[0:1, :] * u2 + cw_ref[1:2, :] * u1) + cw_ref[2:3, :] * u
    cg = cg_ref[...]
    cin = ((bc_ref[...] * conv) * (cg * _sigmoid(cg))).astype(BF16)
    cbr = jnp.dot(cin, wc_ref[...], preferred_element_type=F32)

    m = _sigmoid(mla_ref[...]) * a + _sigmoid(mlc_ref[...]) * cbr
    y_ref[...] = x_ref[...] + jnp.dot(m.astype(BF16), wo_ref[...], preferred_element_type=F32)


def _merge(x, z, o_att, conv_w, wa, wc, wo, *, tm, rows_per_seq, gate_col, conv_col, state=None):
    m_rows, d = x.shape
    sample = state is not None
    assert m_rows % tm == 0 and (sample or rows_per_seq % tm == 0)
    row = lambda i: (i, 0)
    const = lambda i: (0, 0)
    zcol = lambda j: (lambda i: (i, j))
    resident = dict(pipeline_mode=pl.Buffered(1))
    in_specs = [
        pl.BlockSpec((tm, D_ATT), row),
        pl.BlockSpec((tm, D_ATT), zcol(gate_col)),
        pl.BlockSpec((tm, d), zcol(conv_col)),
        pl.BlockSpec((tm, d), zcol(conv_col + 1)),
        pl.BlockSpec((tm, d), zcol(conv_col + 2)),
        pl.BlockSpec((tm, d), zcol(conv_col + 3)),
        pl.BlockSpec((tm, d), zcol(conv_col + 4)),
        pl.BlockSpec((tm, d), zcol(conv_col + 5)),
        pl.BlockSpec((tm, d), row),
        pl.BlockSpec((CONV_WIDTH, d), const),
        pl.BlockSpec((D_ATT, d), const, **resident),
        pl.BlockSpec((d, d), const, **resident),
        pl.BlockSpec((d, d), const, **resident),
    ]
    args = [o_att, z, z, z, z, z, z, z, x, conv_w, wa, wc, wo]
    if sample:
        in_specs += [pl.BlockSpec((tm, d), row), pl.BlockSpec((tm, d), row)]
        args += list(state)
        u_spec = pl.BlockSpec((tm, d), row)
        u_shape = jax.ShapeDtypeStruct((m_rows, d), F32)
    else:
        per_seq = rows_per_seq // tm
        u_spec = pl.BlockSpec((None, SUBLANES, d), lambda i: (i // per_seq, 0, 0))
        u_shape = jax.ShapeDtypeStruct((m_rows // rows_per_seq, SUBLANES, d), F32)
    est = (2 * tm * (2 * D_ATT + 8 * d) * 4 + (D_ATT * d + 2 * d * d) * 2
           + 10 * tm * d * 4 + (4 << 20))
    return pl.pallas_call(
        functools.partial(_merge_body, sample=sample, rows_per_seq=rows_per_seq),
        grid=(m_rows // tm,),
        in_specs=in_specs,
        out_specs=[pl.BlockSpec((tm, d), row), u_spec],
        out_shape=[jax.ShapeDtypeStruct((m_rows, d), F32), u_shape],
        scratch_shapes=[pltpu.VMEM((tm + 2 * SUBLANES, d), F32)],
        compiler_params=pltpu.CompilerParams(
            dimension_semantics=("arbitrary",), vmem_limit_bytes=_vmem_limit(est)),
        name="merge_sample" if sample else "merge_prompt",
    )(*args)


def _layer(xp, xs, caches, state, norm_w, w_in, q_norm_w, k_norm_w, conv_w, w_att_proj, w_conv_proj, w_out):
    b, s, d = xp.shape
    db, t_new, _ = xs.shape
    n = w_in.shape[1]
    assert n == N_QKV * D_ATT + D_ATT + 6 * d and d % D_ATT == 0
    w_in_b = w_in.astype(BF16)
    wa, wc, wo = w_att_proj.astype(BF16), w_conv_proj.astype(BF16), w_out.astype(BF16)
    g = norm_w.reshape(1, d)
    xp2, xs2 = xp.reshape(b * s, d), xs.reshape(db * t_new, d)

    rows_s = db * t_new
    zs = _proj_plain(_rmsnorm(xs2, g, rows_s), w_in_b, TILE)
    rope_s = _rope_tables(PAST_LEN + jnp.arange(t_new, dtype=F32))
    qkv_s = zs[:, :N_QKV * D_ATT].reshape(db, t_new, N_QKV, N_HEADS, HEAD_DIM)
    flat_caches = [c.reshape(db, c.shape[1], 2 * N_HEADS, HEAD_DIM) for c in caches]
    o_att_s, *kv_new = _attn_sample(qkv_s, flat_caches, rope_s, q_norm_w, k_norm_w)
    p1 = jnp.pad(state[:, 1:2], ((0, 0), (0, t_new - 1), (0, 0))).reshape(rows_s, d)
    p2 = jnp.pad(state, ((0, 0), (0, t_new - 2), (0, 0))).reshape(rows_s, d)
    ys, u_s = _merge(xs2, zs, o_att_s.reshape(rows_s, D_ATT), conv_w, wa, wc, wo, tm=rows_s, rows_per_seq=t_new,
                     gate_col=N_QKV, conv_col=(N_QKV * D_ATT + D_ATT) // d, state=(p1, p2))
    conv_s = u_s.reshape(db, t_new, d)[:, t_new - (CONV_WIDTH - 1):]

    rope_p = _rope_tables(jnp.arange(s, dtype=F32))
    qkv_p, rest_p, tails, rolled = _proj_prompt(_rmsnorm(xp2, g, 256), w_in_b, rope_p, q_norm_w, k_norm_w,
                                                flat_caches, kv_new, b, s)
    kv_s = [c.reshape(db, -1, 2, N_HEADS, HEAD_DIM) for c in rolled]
    kv_p = [jnp.transpose(t, (0, 3, 1, 2, 4)) for t in tails]
    o_att_p = _attn_prompt(qkv_p, b, s)
    yp, u_tail = _merge(xp2, rest_p, o_att_p, conv_w, wa, wc, wo, tm=128, rows_per_seq=s,
                        gate_col=6 * d // D_ATT, conv_col=0)
    conv_p = u_tail[:, SUBLANES - (CONV_WIDTH - 1):]

    return yp.reshape(b, s, d), ys.reshape(db, t_new, d), kv_p, conv_p, kv_s, conv_s


def kernel(x_prompt, x_sample, cache_kv_w128, cache_kv_w512, cache_kv_w2048, state_conv, norm_w, w_in, q_norm_w,
           k_norm_w, conv_w, w_att_proj, w_conv_proj, w_out):
    depth = norm_w.shape[0]
    caches = (cache_kv_w128, cache_kv_w512, cache_kv_w2048)
    yp, ys = x_prompt, x_sample
    kv_p = [[] for _ in GROUPS]
    kv_s = [[] for _ in GROUPS]
    conv_p, conv_s = [], []
    for l in range(depth):
        yp, ys, kvp_l, cp_l, kvs_l, cs_l = _layer(
            yp, ys, [c[l] for c in caches], state_conv[l], norm_w[l], w_in[l], q_norm_w[l], k_norm_w[l],
            conv_w[l], w_att_proj[l], w_conv_proj[l], w_out[l])
        for gi in range(N_GROUPS):
            kv_p[gi].append(kvp_l[gi])
            kv_s[gi].append(kvs_l[gi])
        conv_p.append(cp_l)
        conv_s.append(cs_l)
    stack = lambda xs: xs[0][None] if len(xs) == 1 else jnp.stack(xs, axis=0)
    return (yp, ys, stack(kv_p[0]), stack(kv_p[1]), stack(kv_p[2]), stack(conv_p),
            stack(kv_s[0]), stack(kv_s[1]), stack(kv_s[2]), stack(conv_s))
```

```python
import functools

import jax
import jax.numpy as jnp
from jax import lax
from jax.experimental import pallas as pl
from jax.experimental.pallas import tpu as pltpu

F32 = jnp.float32
BF16 = jnp.bfloat16

HEAD_DIM = 128
N_HEADS = 8
GROUPS = ((128, 1), (512, 4), (2048, 16))
N_GROUPS = len(GROUPS)
N_QKV = 3 * N_GROUPS
BAND = 128
D_ATT = N_HEADS * HEAD_DIM
ROT_DIM = HEAD_DIM // 4
ROT_HALF = ROT_DIM // 2
ROPE_THETA = 500000.0
EPS = 1e-6
SCALE = HEAD_DIM ** -0.5
NEG_INF = -1e30
PAST_LEN = 16384
CONV_WIDTH = 3

V7X_VMEM_BYTES = 64 * 1024 * 1024
SUBLANES = 8

TILE = 1024
SUPER = BAND * max(d for _, d in GROUPS)
TILES_PER_SUPER = SUPER // TILE


def _vmem_limit(estimate_bytes):
    return int(min(V7X_VMEM_BYTES - 8 * 1024 * 1024, max(32 * 1024 * 1024, estimate_bytes)))


def _rmsnorm_body(x_ref, g_ref, o_ref):
    x = x_ref[...]
    ms = jnp.mean(x * x, axis=-1, keepdims=True)
    o_ref[...] = ((x * lax.rsqrt(ms + EPS)) * g_ref[...]).astype(BF16)


def _rmsnorm(x, g, tm):
    m, d = x.shape
    assert m % tm == 0
    return pl.pallas_call(
        _rmsnorm_body,
        grid=(m // tm,),
        in_specs=[pl.BlockSpec((tm, d), lambda i: (i, 0)), pl.BlockSpec((1, d), lambda i: (0, 0))],
        out_specs=pl.BlockSpec((tm, d), lambda i: (i, 0)),
        out_shape=jax.ShapeDtypeStruct((m, d), BF16),
        compiler_params=pltpu.CompilerParams(dimension_semantics=("arbitrary",)),
        name="rmsnorm",
    )(x, g)


def _proj_plain_body(x_ref, w_ref, o_ref):
    o_ref[...] = jnp.dot(x_ref[...], w_ref[...], preferred_element_type=F32)


def _proj_plain(xn, w_bf16, tn):
    m, d = xn.shape
    n = w_bf16.shape[1]
    assert n % tn == 0
    return pl.pallas_call(
        _proj_plain_body,
        grid=(n // tn,),
        in_specs=[pl.BlockSpec((m, d), lambda j: (0, 0)), pl.BlockSpec((d, tn), lambda j: (0, j))],
        out_specs=pl.BlockSpec((m, tn), lambda j: (0, j)),
        out_shape=jax.ShapeDtypeStruct((m, n), F32),
        compiler_params=pltpu.CompilerParams(dimension_semantics=("arbitrary",)),
        name="proj_plain",
    )(xn, w_bf16)


def _rope_tables(pos):
    n = pos.shape[0]
    inv = ROPE_THETA ** (-jnp.arange(ROT_HALF, dtype=F32) * (2.0 / ROT_DIM))
    ang = pos[:, None] * inv[None, :]
    cos, sin = jnp.cos(ang), jnp.sin(ang)
    c = jnp.concatenate([cos, cos, jnp.ones((n, HEAD_DIM - ROT_DIM), F32)], axis=1)
    sa = jnp.concatenate([-sin, jnp.zeros((n, HEAD_DIM - ROT_HALF), F32)], axis=1)
    sb = jnp.concatenate([jnp.zeros((n, ROT_HALF), F32), sin, jnp.zeros((n, HEAD_DIM - ROT_DIM), F32)], axis=1)
    return jnp.stack([c, sa, sb], axis=0)


def _norm_rope(x, w, c, sa, sb):
    r = lax.rsqrt(jnp.mean(x * x, axis=-1, keepdims=True) + EPS)
    xn = (x * r) * w
    up = pltpu.roll(xn, HEAD_DIM - ROT_HALF, 1)
    dn = pltpu.roll(xn, ROT_HALF, 1)
    return xn * c + up * sa + dn * sb


def _group_weights(l0, l1, l2):
    mx = jnp.maximum(jnp.maximum(l0, l1), l2)
    e0, e1, e2 = jnp.exp(l0 - mx), jnp.exp(l1 - mx), jnp.exp(l2 - mx)
    den = (e0 + e1) + e2
    return e0 / den, e1 / den, e2 / den


ROLL_ROWS = 300
ROLL_BUFFERS = 4
ROLL_LAG = 2


def _roll_plan(caches, shift):
    kinds, first = [], 0
    for g, c in enumerate(caches):
        db, window = c.shape[:2]
        keep = window - shift
        rows = max(r for r in range(1, ROLL_ROWS + 1) if keep % r == 0)
        kinds.append((g, rows, keep // rows, first, db * (keep // rows)))
        first += kinds[-1][4]
    return tuple(kinds), first


def _roll_step(s, kinds, shift, caches, news, outs, buf, new_buf, sem_in, sem_out, sem_new):
    def copy(kind, t, inbound):
        g, rows, per_batch, first, _ = kind
        b = (t - first) // per_batch
        c = (t - first) % per_batch
        slot = t % ROLL_BUFFERS
        stage = buf.at[slot, pl.ds(0, rows)]
        if inbound:
            return pltpu.make_async_copy(caches[g].at[b, pl.ds(shift + c * rows, rows)], stage, sem_in.at[slot])
        return pltpu.make_async_copy(stage, outs[g].at[b, pl.ds(c * rows, rows)], sem_out.at[slot])

    def for_item(t, fn):
        for kind in kinds:
            pl.when(jnp.logical_and(t >= kind[3], t < kind[3] + kind[4]))(functools.partial(fn, kind, t))

    def turn_around(kind, t):
        copy(kind, t, True).wait()
        copy(kind, t, False).start()

    for_item(s - ROLL_BUFFERS, lambda kind, t: copy(kind, t, False).wait())
    for_item(s, lambda kind, t: copy(kind, t, True).start())
    for_item(s - ROLL_LAG, turn_around)

    def new_copy(g, inbound):
        keep = outs[g].shape[1] - shift
        if inbound:
            return pltpu.make_async_copy(news[g], new_buf.at[g], sem_new.at[0, g])
        return pltpu.make_async_copy(new_buf.at[g], outs[g].at[:, pl.ds(keep, shift)], sem_new.at[1, g])

    @pl.when(s == 0)
    def _():
        for g in range(len(news)):
            new_copy(g, True).start()

    @pl.when(s == 1)
    def _():
        for g in range(len(news)):
            new_copy(g, True).wait()
            new_copy(g, False).start()

    @pl.when(s == 2)
    def _():
        for g in range(len(news)):
            new_copy(g, False).wait()


COL_CHUNK = 2 * HEAD_DIM


def _finish_head(kind, dil, h, src, w, rope_ref, qkv_ref):
    cols = pl.ds(pl.multiple_of(h * HEAD_DIM, HEAD_DIM), HEAD_DIM)
    per = TILE // dil
    if kind == "qk":
        y = _norm_rope(src[h], w, rope_ref[0], rope_ref[1], rope_ref[2])
        src[h] = y
        if dil == 1:
            qkv_ref[:, cols] = y.astype(BF16)
    elif dil == 1:
        qkv_ref[:, cols] = src[h].astype(BF16)
    if dil > 1:
        for r in range(dil):
            qkv_ref[r * per:(r + 1) * per, cols] = src[h, pl.ds(r, per, stride=dil), :].astype(BF16)


def _proj_prompt_body(*refs, roll, tiles_per_seq, n_rest):
    kinds, shift, nc = roll
    xn_ref, w_ref, rope_ref, qw_ref, kw_ref = refs[:5]
    caches, news = refs[5:5 + nc], refs[5 + nc:5 + 2 * nc]
    qkv_ref, rest_ref = refs[5 + 2 * nc:7 + 2 * nc]
    tails = refs[7 + 2 * nc:7 + 2 * nc + N_GROUPS]
    rolled = refs[7 + 2 * nc + N_GROUPS:7 + 3 * nc + N_GROUPS]
    res, buf, new_buf, sem_in, sem_out, sem_new, sem_tail = refs[7 + 3 * nc + N_GROUPS:]
    i = pl.program_id(0)
    j = pl.program_id(1)
    _roll_step(i * pl.num_programs(1) + j, kinds, shift, caches, news, rolled, buf, new_buf, sem_in, sem_out, sem_new)
    batch = i // tiles_per_seq
    tile_in_seq = i % tiles_per_seq
    slot = j % 2
    heads_per_chunk = COL_CHUNK // HEAD_DIM

    def run(to_res, finish):
        if finish is not None:
            kind, g = finish
            window, dil = GROUPS[g]
            src = res.at[1 - slot]
            is_q = j == 3 * g + 1
            w = jnp.where(is_q, qw_ref[g:g + 1, :], kw_ref[g:g + 1, :]) if kind == "qk" else None

        def chunk(c, carry):
            if finish is not None:
                for hh in range(heads_per_chunk):
                    _finish_head(kind, dil, c * heads_per_chunk + hh, src, w, rope_ref, qkv_ref)
            col0 = pl.multiple_of(c * COL_CHUNK, COL_CHUNK)
            z = jnp.dot(xn_ref[...], w_ref[:, pl.ds(col0, COL_CHUNK)], preferred_element_type=F32)
            if to_res:
                for hh in range(heads_per_chunk):
                    res[slot, c * heads_per_chunk + hh] = z[:, hh * HEAD_DIM:(hh + 1) * HEAD_DIM]
            else:
                rest_ref[:, pl.ds(col0, COL_CHUNK)] = z
            return carry
        lax.fori_loop(0, TILE // COL_CHUNK, chunk, 0)

        if finish is None:
            return
        n_tail_tiles = -(-window // TILE)
        tail_rows = min(window, TILE)
        first_tail_tile = tiles_per_seq - n_tail_tiles
        writes_tail = tile_in_seq >= first_tail_tile
        if kind == "qk":
            writes_tail = jnp.logical_and(writes_tail, jnp.logical_not(is_q))

        @pl.when(writes_tail)
        def _tail():
            dst_row = (tile_in_seq - first_tail_tile) * tail_rows
            cp = pltpu.make_async_copy(
                src.at[:, pl.ds(TILE - tail_rows, tail_rows)],
                tails[g].at[batch, 0 if kind == "qk" else 1, :, pl.ds(dst_row, tail_rows)], sem_tail.at[0])
            cp.start()
            cp.wait()

    pl.when(j == 0)(lambda: run(True, None))
    for g in range(N_GROUPS):
        pl.when(jnp.logical_or(j == 3 * g + 1, j == 3 * g + 2))(functools.partial(run, True, ("qk", g)))
        pl.when(j == 3 * g + 3)(functools.partial(run, 3 * g + 3 < N_QKV, ("v", g)))
    pl.when(j > N_QKV)(lambda: run(False, None))


def _proj_prompt(xn, w_bf16, rope, qw, kw, caches, news, batch, seq):
    m, d = xn.shape
    n = w_bf16.shape[1]
    assert m == batch * seq and seq % TILE == 0 and n % TILE == 0 and D_ATT == TILE
    tiles_per_seq = seq // TILE
    nj = n // TILE
    n_rest = nj - N_QKV
    grid = (m // TILE, nj)
    nc = len(caches)
    shift = news[0].shape[1]
    kinds, n_items = _roll_plan(caches, shift)
    assert n_items + ROLL_BUFFERS < grid[0] * grid[1] and all(nw.shape == news[0].shape for nw in news)
    assert all(w % TILE == 0 or TILE % w == 0 for w, _ in GROUPS)

    def w_col(i, j):
        return (0, jnp.where(j < N_QKV, j, jnp.where(j < nj - 1, j + 1, N_QKV)))

    any_spec = pl.BlockSpec(memory_space=pl.ANY)
    in_specs = [
        pl.BlockSpec((TILE, d), lambda i, j: (i, 0)),
        pl.BlockSpec((d, TILE), w_col),
        pl.BlockSpec((3, TILE, HEAD_DIM), lambda i, j: (0, i % tiles_per_seq, 0)),
        pl.BlockSpec((N_GROUPS, HEAD_DIM), lambda i, j: (0, 0)),
        pl.BlockSpec((N_GROUPS, HEAD_DIM), lambda i, j: (0, 0)),
        *([any_spec] * (2 * nc)),
    ]
    out_specs = [
        pl.BlockSpec((None, None, TILE, TILE), lambda i, j: (jnp.clip(j - 1, 0, N_QKV - 1), i, 0, 0)),
        pl.BlockSpec((TILE, TILE), lambda i, j: (i, jnp.clip(j - N_QKV, 0, n_rest - 1))),
        *([any_spec] * (N_GROUPS + nc)),
    ]
    out_shape = [
        jax.ShapeDtypeStruct((N_QKV, m // TILE, TILE, TILE), BF16),
        jax.ShapeDtypeStruct((m, n_rest * TILE), F32),
        *[jax.ShapeDtypeStruct((batch, 2, N_HEADS, w, HEAD_DIM), F32) for w, _ in GROUPS],
        *[jax.ShapeDtypeStruct(c.shape, c.dtype) for c in caches],
    ]
    row_shape = caches[0].shape[2:]
    max_rows = max(k[1] for k in kinds)
    scratch = [
        pltpu.VMEM((2, N_HEADS, TILE, HEAD_DIM), F32),
        pltpu.VMEM((ROLL_BUFFERS, max_rows) + row_shape, F32),
        pltpu.VMEM((nc,) + news[0].shape, F32),
        pltpu.SemaphoreType.DMA((ROLL_BUFFERS,)),
        pltpu.SemaphoreType.DMA((ROLL_BUFFERS,)),
        pltpu.SemaphoreType.DMA((2, nc)),
        pltpu.SemaphoreType.DMA((1,)),
    ]
    row_bytes = 4 * row_shape[0] * row_shape[1]
    est = (2 * TILE * d * 2 + 2 * d * TILE * 2 + 2 * TILE * 3 * HEAD_DIM * 4 + 2 * TILE * TILE * 2 + 2 * TILE * TILE * 4
           + 2 * TILE * TILE * 4 + (ROLL_BUFFERS * max_rows + nc * news[0].shape[0] * shift) * row_bytes + (6 << 20))
    out = pl.pallas_call(
        functools.partial(_proj_prompt_body, roll=(kinds, shift, nc), tiles_per_seq=tiles_per_seq, n_rest=n_rest),
        grid=grid,
        in_specs=in_specs,
        out_specs=out_specs,
        out_shape=out_shape,
        scratch_shapes=scratch,
        compiler_params=pltpu.CompilerParams(
            dimension_semantics=("arbitrary", "arbitrary"),
            vmem_limit_bytes=_vmem_limit(est)),
        name="proj_prompt",
    )(xn, w_bf16, rope, qw, kw, *caches, *news)
    return out[0], out[1], out[2:2 + N_GROUPS], out[2 + N_GROUPS:]


def _subsequence_rows(cur_ref, prev_ref, dil, res, l0, n):
    per = TILE // dil
    pieces = []
    l = l0
    while l < l0 + n:
        ll = l + TILES_PER_SUPER * per if l < 0 else l
        t, off = divmod(ll, per)
        stop = 0 if l < 0 else l0 + n
        take = min(per - off, stop - l)
        if l < 0:
            ref, t = prev_ref, t - (TILES_PER_SUPER - prev_ref.shape[0])
        else:
            ref = cur_ref
        pieces.append(ref[t, res * per + off:res * per + off + take, :])
        l += take
    return pieces[0] if len(pieces) == 1 else jnp.concatenate(pieces, axis=0)


def _attn_prompt_body(*refs):
    cur = refs[:N_QKV]
    prev = refs[N_QKV:N_QKV + 2 * N_GROUPS]
    o_ref, obuf, lbuf = refs[N_QKV + 2 * N_GROUPS:]
    sb = pl.program_id(2)
    qi = lax.broadcasted_iota(jnp.int32, (BAND, 2 * BAND), 0)
    kj = lax.broadcasted_iota(jnp.int32, (BAND, 2 * BAND), 1)
    hi = qi + BAND
    lo_first = jnp.maximum(qi, jnp.where(sb == 0, BAND, 0))
    nt = (((1,), (1,)), ((), ()))
    for g, (_, dil) in enumerate(GROUPS):
        q_ref, k_ref, v_ref = cur[3 * g:3 * g + 3]
        kp_ref, vp_ref = prev[2 * g:2 * g + 2]
        for res in range(dil):
            for blk in range(SUPER // (BAND * dil)):
                l0 = BAND * blk
                q = _subsequence_rows(q_ref, None, dil, res, l0, BAND)
                kc = _subsequence_rows(k_ref, kp_ref, dil, res, l0 - BAND, 2 * BAND)
                vc = _subsequence_rows(v_ref, vp_ref, dil, res, l0 - BAND, 2 * BAND)
                s = lax.dot_general(q, kc, nt, preferred_element_type=F32) * SCALE
                lo = lo_first if blk == 0 else qi
                s = jnp.where(kj >= lo, jnp.where(kj <= hi, s, NEG_INF), NEG_INF)
                m = jnp.max(s, axis=-1, keepdims=True)
                p = jnp.exp(s - m)
                l = jnp.sum(p, axis=-1, keepdims=True)
                o = jnp.dot((p / l).astype(BF16), vc, preferred_element_type=F32)
                start = dil * l0 + res
                rows = pl.ds(start, BAND, stride=dil) if dil > 1 else pl.ds(start, BAND)
                obuf[g, rows, :] = o
                lbuf[g, rows, :] = jnp.broadcast_to(m + jnp.log(l), (BAND, HEAD_DIM))

    rows_per_step = 256

    def combine(i, carry):
        rows = pl.ds(pl.multiple_of(i * rows_per_step, rows_per_step), rows_per_step)
        w0, w1, w2 = _group_weights(lbuf[0, rows, :], lbuf[1, rows, :], lbuf[2, rows, :])
        o_ref[rows, :] = (w0 * obuf[0, rows, :] + w1 * obuf[1, rows, :]) + w2 * obuf[2, rows, :]
        return carry
    lax.fori_loop(0, SUPER // rows_per_step, combine, 0)


def _attn_prompt(qkv, batch, seq):
    assert seq % SUPER == 0 and SUPER % TILE == 0
    nsb = seq // SUPER
    tiles_per_seq = seq // TILE
    in_specs = [
        pl.BlockSpec((None, TILES_PER_SUPER, TILE, HEAD_DIM), lambda b, h, sb, jj=jj: (jj, b * nsb + sb, 0, h))
        for jj in range(N_QKV)
    ]
    args = [qkv] * N_QKV
    prev_bytes = 0
    for g, (_, dil) in enumerate(GROUPS):
        n_prev = -(-BAND * dil // TILE)
        assert TILES_PER_SUPER % n_prev == 0
        per_super = TILES_PER_SUPER // n_prev
        for which in (1, 2):
            in_specs.append(pl.BlockSpec(
                (None, n_prev, TILE, HEAD_DIM),
                lambda b, h, sb, jj=3 * g + which, per_super=per_super:
                    (jj, jnp.maximum((b * nsb + sb) * per_super - 1, b * nsb * per_super), 0, h)))
            args.append(qkv)
            prev_bytes += n_prev * TILE * HEAD_DIM * 2
    est = (2 * (N_QKV * TILES_PER_SUPER * TILE * HEAD_DIM * 2 + prev_bytes + SUPER * HEAD_DIM * 4)
           + 2 * N_GROUPS * SUPER * HEAD_DIM * 4 + (12 << 20))
    return pl.pallas_call(
        _attn_prompt_body,
        grid=(batch, N_HEADS, nsb),
        in_specs=in_specs,
        out_specs=pl.BlockSpec((SUPER, HEAD_DIM), lambda b, h, sb: (b * nsb + sb, h)),
        out_shape=jax.ShapeDtypeStruct((batch * seq, D_ATT), F32),
        scratch_shapes=[
            pltpu.VMEM((N_GROUPS, SUPER, HEAD_DIM), F32),
            pltpu.VMEM((N_GROUPS, SUPER, HEAD_DIM), F32),
        ],
        compiler_params=pltpu.CompilerParams(
            dimension_semantics=("arbitrary", "arbitrary", "arbitrary"),
            vmem_limit_bytes=_vmem_limit(est)),
        name="attn_prompt",
    )(*args)


def _attn_sample_body(qkv_ref, c0_ref, c1_ref, c2_ref, rope_ref, qw_ref, kw_ref,
                      o_ref, n0_ref, n1_ref, n2_ref):
    t_new = qkv_ref.shape[0]
    cache_refs = (c0_ref, c1_ref, c2_ref)
    new_refs = (n0_ref, n1_ref, n2_ref)
    outs = [[None] * N_GROUPS for _ in range(t_new)]
    lses = [[None] * N_GROUPS for _ in range(t_new)]
    for g, (_, dil) in enumerate(GROUPS):
        qs, ks, vs = [], [], []
        for t in range(t_new):
            c, sa, sbn = rope_ref[0, t:t + 1, :], rope_ref[1, t:t + 1, :], rope_ref[2, t:t + 1, :]
            qs.append(_norm_rope(qkv_ref[t, 3 * g], qw_ref[g:g + 1, :], c, sa, sbn))
            ks.append(_norm_rope(qkv_ref[t, 3 * g + 1], kw_ref[g:g + 1, :], c, sa, sbn))
            vs.append(qkv_ref[t, 3 * g + 2])
            new_refs[g][t, 0:N_HEADS, :] = ks[t]
            new_refs[g][t, N_HEADS:2 * N_HEADS, :] = vs[t]
        for t in range(t_new):
            res = t % dil
            first = -((res - t) // dil)
            kc = cache_refs[g][first:, res, 0:N_HEADS, :]
            vc = cache_refs[g][first:, res, N_HEADS:2 * N_HEADS, :]
            q = qs[t]
            s_c = jnp.sum(kc * q[None], axis=-1, keepdims=True) * SCALE
            new_keys = [tp for tp in range(t + 1) if (t - tp) % dil == 0]
            s_n = [jnp.sum(ks[tp] * q, axis=-1, keepdims=True) * SCALE for tp in new_keys]
            m = jnp.max(s_c, axis=0)
            for s in s_n:
                m = jnp.maximum(m, s)
            p_c = jnp.exp(s_c - m[None])
            p_n = [jnp.exp(s - m) for s in s_n]
            l = jnp.sum(p_c, axis=0)
            for p in p_n:
                l = l + p
            o = jnp.sum((p_c / l[None]) * vc, axis=0)
            for p, tp in zip(p_n, new_keys):
                o = o + (p / l) * vs[tp]
            outs[t][g] = o
            lses[t][g] = m + jnp.log(l)
    for t in range(t_new):
        w0, w1, w2 = _group_weights(*lses[t])
        o_ref[t] = (w0 * outs[t][0] + w1 * outs[t][1]) + w2 * outs[t][2]


def _attn_sample(qkv, caches, rope, qw, kw):
    db, t_new = qkv.shape[:2]
    views, cache_specs, cache_bytes = [], [], 0
    for (window, dil), cache in zip(GROUPS, caches):
        assert cache.shape[1] == window == BAND * dil and t_new <= BAND
        n_res = min(dil, t_new)
        views.append(cache.reshape(db, BAND, dil, 2 * N_HEADS, HEAD_DIM))
        cache_specs.append(pl.BlockSpec((None, BAND, n_res, 2 * N_HEADS, HEAD_DIM), lambda b: (b, 0, 0, 0, 0)))
        cache_bytes += BAND * n_res * 2 * N_HEADS * HEAD_DIM * 4
    new_spec = pl.BlockSpec((None, t_new, 2 * N_HEADS, HEAD_DIM), lambda b: (b, 0, 0, 0))
    new_shape = jax.ShapeDtypeStruct((db, t_new, 2 * N_HEADS, HEAD_DIM), F32)
    return pl.pallas_call(
        _attn_sample_body,
        grid=(db,),
        in_specs=[
            pl.BlockSpec((None, t_new, 3 * N_GROUPS, N_HEADS, HEAD_DIM), lambda b: (b, 0, 0, 0, 0)),
            *cache_specs,
            pl.BlockSpec((3, t_new, HEAD_DIM), lambda b: (0, 0, 0)),
            pl.BlockSpec((N_GROUPS, HEAD_DIM), lambda b: (0, 0)),
            pl.BlockSpec((N_GROUPS, HEAD_DIM), lambda b: (0, 0)),
        ],
        out_specs=[pl.BlockSpec((None, t_new, N_HEADS, HEAD_DIM), lambda b: (b, 0, 0, 0)),
                   new_spec, new_spec, new_spec],
        out_shape=[jax.ShapeDtypeStruct((db, t_new, N_HEADS, HEAD_DIM), F32),
                   new_shape, new_shape, new_shape],
        compiler_params=pltpu.CompilerParams(
            dimension_semantics=("arbitrary",), vmem_limit_bytes=_vmem_limit(2 * cache_bytes + (16 << 20))),
        name="attn_sample",
    )(qkv, *views, rope, qw, kw)


def _sigmoid(x):
    return 1.0 / (1.0 + jnp.exp(-x))


def _merge_body(*refs, sample, rows_per_seq):
    if sample:
        (oa_ref, ag_ref, hc_ref, bc_ref, cc_ref, cg_ref, mla_ref, mlc_ref, x_ref, cw_ref,
         wa_ref, wc_ref, wo_ref, p1_ref, p2_ref, y_ref, u_ref, ubuf_ref) = refs
    else:
        (oa_ref, ag_ref, hc_ref, bc_ref, cc_ref, cg_ref, mla_ref, mlc_ref, x_ref, cw_ref,
         wa_ref, wc_ref, wo_ref, y_ref, u_ref, ubuf_ref) = refs
    tm = x_ref.shape[0]

    ag = ag_ref[...]
    ain = (oa_ref[...] * (ag * _sigmoid(ag))).astype(BF16)
    a = jnp.dot(ain, wa_ref[...], preferred_element_type=F32)

    u = cc_ref[...] * hc_ref[...]
    if sample:
        ubuf_ref[0:SUBLANES, :] = jnp.zeros((SUBLANES, u.shape[1]), F32)
    else:
        @pl.when(pl.program_id(0) % (rows_per_seq // tm) == 0)
        def _sequence_start():
            ubuf_ref[0:SUBLANES, :] = jnp.zeros((SUBLANES, u.shape[1]), F32)
    ubuf_ref[SUBLANES:SUBLANES + tm, :] = u
    u1 = ubuf_ref[SUBLANES - 1:SUBLANES - 1 + tm, :]
    u2 = ubuf_ref[SUBLANES - 2:SUBLANES - 2 + tm, :]
    if sample:
        t = lax.broadcasted_iota(jnp.int32, (tm, 1), 0) % rows_per_seq
        u1 = jnp.where(t < 1, p1_ref[...], u1)
        u2 = jnp.where(t < 2, p2_ref[...], u2)
        u_ref[...] = u
    else:
        tail = ubuf_ref[tm:tm + SUBLANES, :]
        ubuf_ref[0:SUBLANES, :] = tail
        u_ref[...] = tail
    conv = (cw_ref[0:1, :] * u2 + cw_ref[1:2, :] * u1) + cw_ref[2:3, :] * u
    cg = cg_ref[...]
    cin = ((bc_ref[...] * conv) * (cg * _sigmoid(cg))).astype(BF16)
    cbr = jnp.dot(cin, wc_ref[...], preferred_element_type=F32)

    m = _sigmoid(mla_ref[...]) * a + _sigmoid(mlc_ref[...]) * cbr
    y_ref[...] = x_ref[...] + jnp.dot(m.astype(BF16), wo_ref[...], preferred_element_type=F32)


def _merge(x, z, o_att, conv_w, wa, wc, wo, *, tm, rows_per_seq, gate_col, conv_col, state=None):
    m_rows, d = x.shape
    sample = state is not None
    assert m_rows % tm == 0 and (sample or rows_per_seq % tm == 0)
    row = lambda i: (i, 0)
    const = lambda i: (0, 0)
    zcol = lambda j: (lambda i: (i, j))
    resident = dict(pipeline_mode=pl.Buffered(1))
    in_specs = [
        pl.BlockSpec((tm, D_ATT), row),
        pl.BlockSpec((tm, D_ATT), zcol(gate_col)),
        pl.BlockSpec((tm, d), zcol(conv_col)),
        pl.BlockSpec((tm, d), zcol(conv_col + 1)),
        pl.BlockSpec((tm, d), zcol(conv_col + 2)),
        pl.BlockSpec((tm, d), zcol(conv_col + 3)),
        pl.BlockSpec((tm, d), zcol(conv_col + 4)),
        pl.BlockSpec((tm, d), zcol(conv_col + 5)),
        pl.BlockSpec((tm, d), row),
        pl.BlockSpec((CONV_WIDTH, d), const),
        pl.BlockSpec((D_ATT, d), const, **resident),
        pl.BlockSpec((d, d), const, **resident),
        pl.BlockSpec((d, d), const, **resident),
    ]
    args = [o_att, z, z, z, z, z, z, z, x, conv_w, wa, wc, wo]
    if sample:
        in_specs += [pl.BlockSpec((tm, d), row), pl.BlockSpec((tm, d), row)]
        args += list(state)
        u_spec = pl.BlockSpec((tm, d), row)
        u_shape = jax.ShapeDtypeStruct((m_rows, d), F32)
    else:
        per_seq = rows_per_seq // tm
        u_spec = pl.BlockSpec((None, SUBLANES, d), lambda i: (i // per_seq, 0, 0))
        u_shape = jax.ShapeDtypeStruct((m_rows // rows_per_seq, SUBLANES, d), F32)
    est = (2 * tm * (2 * D_ATT + 8 * d) * 4 + (D_ATT * d + 2 * d * d) * 2
           + 10 * tm * d * 4 + (4 << 20))
    return pl.pallas_call(
        functools.partial(_merge_body, sample=sample, rows_per_seq=rows_per_seq),
        grid=(m_rows // tm,),
        in_specs=in_specs,
        out_specs=[pl.BlockSpec((tm, d), row), u_spec],
        out_shape=[jax.ShapeDtypeStruct((m_rows, d), F32), u_shape],
        scratch_shapes=[pltpu.VMEM((tm + 2 * SUBLANES, d), F32)],
        compiler_params=pltpu.CompilerParams(
            dimension_semantics=("arbitrary",), vmem_limit_bytes=_vmem_limit(est)),
        name="merge_sample" if sample else "merge_prompt",
    )(*args)


def _layer(xp, xs, caches, state, norm_w, w_in, q_norm_w, k_norm_w, conv_w, w_att_proj, w_conv_proj, w_out):
    b, s, d = xp.shape
    db, t_new, _ = xs.shape
    n = w_in.shape[1]
    assert n == N_QKV * D_ATT + D_ATT + 6 * d and d % D_ATT == 0
    w_in_b = w_in.astype(BF16)
    wa, wc, wo = w_att_proj.astype(BF16), w_conv_proj.astype(BF16), w_out.astype(BF16)
    g = norm_w.reshape(1, d)
    xp2, xs2 = xp.reshape(b * s, d), xs.reshape(db * t_new, d)

    rows_s = db * t_new
    zs = _proj_plain(_rmsnorm(xs2, g, rows_s), w_in_b, TILE)
    rope_s = _rope_tables(PAST_LEN + jnp.arange(t_new, dtype=F32))
    qkv_s = zs[:, :N_QKV * D_ATT].reshape(db, t_new, N_QKV, N_HEADS, HEAD_DIM)
    flat_caches = [c.reshape(db, c.shape[1], 2 * N_HEADS, HEAD_DIM) for c in caches]
    o_att_s, *kv_new = _attn_sample(qkv_s, flat_caches, rope_s, q_norm_w, k_norm_w)
    p1 = jnp.pad(state[:, 1:2], ((0, 0), (0, t_new - 1), (0, 0))).reshape(rows_s, d)
    p2 = jnp.pad(state, ((0, 0), (0, t_new - 2), (0, 0))).reshape(rows_s, d)
    ys, u_s = _merge(xs2, zs, o_att_s.reshape(rows_s, D_ATT), conv_w, wa, wc, wo, tm=rows_s, rows_per_seq=t_new,
                     gate_col=N_QKV, conv_col=(N_QKV * D_ATT + D_ATT) // d, state=(p1, p2))
    conv_s = u_s.reshape(db, t_new, d)[:, t_new - (CONV_WIDTH - 1):]

    rope_p = _rope_tables(jnp.arange(s, dtype=F32))
    qkv_p, rest_p, tails, rolled = _proj_prompt(_rmsnorm(xp2, g, 256), w_in_b, rope_p, q_norm_w, k_norm_w,
                                                flat_caches, kv_new, b, s)
    kv_s = [c.reshape(db, -1, 2, N_HEADS, HEAD_DIM) for c in rolled]
    kv_p = [jnp.transpose(t, (0, 3, 1, 2, 4)) for t in tails]
    o_att_p = _attn_prompt(qkv_p, b, s)
    yp, u_tail = _merge(xp2, rest_p, o_att_p, conv_w, wa, wc, wo, tm=128, rows_per_seq=s,
                        gate_col=6 * d // D_ATT, conv_col=0)
    conv_p = u_tail[:, SUBLANES - (CONV_WIDTH - 1):]

    return yp.reshape(b, s, d), ys.reshape(db, t_new, d), kv_p, conv_p, kv_s, conv_s


def kernel(x_prompt, x_sample, cache_kv_w128, cache_kv_w512, cache_kv_w2048, state_conv, norm_w, w_in, q_norm_w,
           k_norm_w, conv_w, w_att_proj, w_conv_proj, w_out):
    depth = norm_w.shape[0]
    caches = (cache_kv_w128, cache_kv_w512, cache_kv_w2048)
    yp, ys = x_prompt, x_sample
    kv_p = [[] for _ in GROUPS]
    kv_s = [[] for _ in GROUPS]
    conv_p, conv_s = [], []
    for l in range(depth):
        yp, ys, kvp_l, cp_l, kvs_l, cs_l = _layer(
            yp, ys, [c[l] for c in caches], state_conv[l], norm_w[l], w_in[l], q_norm_w[l], k_norm_w[l],
            conv_w[l], w_att_proj[l], w_conv_proj[l], w_out[l])
        for gi in range(N_GROUPS):
            kv_p[gi].append(kvp_l[gi])
            kv_s[gi].append(kvs_l[gi])
        conv_p.append(cp_l)
        conv_s.append(cs_l)
    stack = lambda xs: xs[0][None] if len(xs) == 1 else jnp.stack(xs, axis=0)
    return (yp, ys, stack(kv_p[0]), stack(kv_p[1]), stack(kv_p[2]), stack(conv_p),
            stack(kv_s[0]), stack(kv_s[1]), stack(kv_s[2]), stack(conv_s))
```

```python
import functools

import jax
import jax.numpy as jnp
from jax import lax
from jax.experimental import pallas as pl
from jax.experimental.pallas import tpu as pltpu

F32 = jnp.float32
BF16 = jnp.bfloat16

HEAD_DIM = 128
N_HEADS = 8
GROUPS = ((128, 1), (512, 4), (2048, 16))
N_GROUPS = len(GROUPS)
N_QKV = 3 * N_GROUPS
BAND = 128
D_ATT = N_HEADS * HEAD_DIM
ROT_DIM = HEAD_DIM // 4
ROT_HALF = ROT_DIM // 2
ROPE_THETA = 500000.0
EPS = 1e-6
SCALE = HEAD_DIM ** -0.5
NEG_INF = -1e30
PAST_LEN = 16384
CONV_WIDTH = 3

V7X_VMEM_BYTES = 64 * 1024 * 1024
SUBLANES = 8

TILE = 1024
SUPER = BAND * max(d for _, d in GROUPS)
TILES_PER_SUPER = SUPER // TILE


def _vmem_limit(estimate_bytes):
    return int(min(V7X_VMEM_BYTES - 8 * 1024 * 1024, max(32 * 1024 * 1024, estimate_bytes)))


def _rmsnorm_body(x_ref, g_ref, o_ref):
    x = x_ref[...]
    ms = jnp.mean(x * x, axis=-1, keepdims=True)
    o_ref[...] = ((x * lax.rsqrt(ms + EPS)) * g_ref[...]).astype(BF16)


def _rmsnorm(x, g, tm):
    m, d = x.shape
    assert m % tm == 0
    return pl.pallas_call(
        _rmsnorm_body,
        grid=(m // tm,),
        in_specs=[pl.BlockSpec((tm, d), lambda i: (i, 0)), pl.BlockSpec((1, d), lambda i: (0, 0))],
        out_specs=pl.BlockSpec((tm, d), lambda i: (i, 0)),
        out_shape=jax.ShapeDtypeStruct((m, d), BF16),
        compiler_params=pltpu.CompilerParams(dimension_semantics=("arbitrary",)),
        name="rmsnorm",
    )(x, g)


def _proj_plain_body(x_ref, w_ref, o_ref):
    o_ref[...] = jnp.dot(x_ref[...], w_ref[...], preferred_element_type=F32)


def _proj_plain(xn, w_bf16, tn):
    m, d = xn.shape
    n = w_bf16.shape[1]
    assert n % tn == 0
    return pl.pallas_call(
        _proj_plain_body,
        grid=(n // tn,),
        in_specs=[pl.BlockSpec((m, d), lambda j: (0, 0)), pl.BlockSpec((d, tn), lambda j: (0, j))],
        out_specs=pl.BlockSpec((m, tn), lambda j: (0, j)),
        out_shape=jax.ShapeDtypeStruct((m, n), F32),
        compiler_params=pltpu.CompilerParams(dimension_semantics=("arbitrary",)),
        name="proj_plain",
    )(xn, w_bf16)


def _rope_tables(pos):
    n = pos.shape[0]
    inv = ROPE_THETA ** (-jnp.arange(ROT_HALF, dtype=F32) * (2.0 / ROT_DIM))
    ang = pos[:, None] * inv[None, :]
    cos, sin = jnp.cos(ang), jnp.sin(ang)
    c = jnp.concatenate([cos, cos, jnp.ones((n, HEAD_DIM - ROT_DIM), F32)], axis=1)
    sa = jnp.concatenate([-sin, jnp.zeros((n, HEAD_DIM - ROT_HALF), F32)], axis=1)
    sb = jnp.concatenate([jnp.zeros((n, ROT_HALF), F32), sin, jnp.zeros((n, HEAD_DIM - ROT_DIM), F32)], axis=1)
    return jnp.stack([c, sa, sb], axis=0)


def _norm_rope(x, w, c, sa, sb):
    r = lax.rsqrt(jnp.mean(x * x, axis=-1, keepdims=True) + EPS)
    xn = (x * r) * w
    up = pltpu.roll(xn, HEAD_DIM - ROT_HALF, 1)
    dn = pltpu.roll(xn, ROT_HALF, 1)
    return xn * c + up * sa + dn * sb


def _group_weights(l0, l1, l2):
    mx = jnp.maximum(jnp.maximum(l0, l1), l2)
    e0, e1, e2 = jnp.exp(l0 - mx), jnp.exp(l1 - mx), jnp.exp(l2 - mx)
    den = (e0 + e1) + e2
    return e0 / den, e1 / den, e2 / den


ROLL_ROWS = 300
ROLL_BUFFERS = 4
ROLL_LAG = 2


def _roll_plan(caches, shift):
    kinds, first = [], 0
    for g, c in enumerate(caches):
        db, window = c.shape[:2]
        keep = window - shift
        rows = max(r for r in range(1, ROLL_ROWS + 1) if keep % r == 0)
        kinds.append((g, rows, keep // rows, first, db * (keep // rows)))
        first += kinds[-1][4]
    return tuple(kinds), first


def _roll_step(s, kinds, shift, caches, news, outs, buf, new_buf, sem_in, sem_out, sem_new):
    def copy(kind, t, inbound):
        g, rows, per_batch, first, _ = kind
        b = (t - first) // per_batch
        c = (t - first) % per_batch
        slot = t % ROLL_BUFFERS
        stage = buf.at[slot, pl.ds(0, rows)]
        if inbound:
            return pltpu.make_async_copy(caches[g].at[b, pl.ds(shift + c * rows, rows)], stage, sem_in.at[slot])
        return pltpu.make_async_copy(stage, outs[g].at[b, pl.ds(c * rows, rows)], sem_out.at[slot])

    def for_item(t, fn):
        for kind in kinds:
            pl.when(jnp.logical_and(t >= kind[3], t < kind[3] + kind[4]))(functools.partial(fn, kind, t))

    def turn_around(kind, t):
        copy(kind, t, True).wait()
        copy(kind, t, False).start()

    for_item(s - ROLL_BUFFERS, lambda kind, t: copy(kind, t, False).wait())
    for_item(s, lambda kind, t: copy(kind, t, True).start())
    for_item(s - ROLL_LAG, turn_around)

    def new_copy(g, inbound):
        keep = outs[g].shape[1] - shift
        if inbound:
            return pltpu.make_async_copy(news[g], new_buf.at[g], sem_new.at[0, g])
        return pltpu.make_async_copy(new_buf.at[g], outs[g].at[:, pl.ds(keep, shift)], sem_new.at[1, g])

    @pl.when(s == 0)
    def _():
        for g in range(len(news)):
            new_copy(g, True).start()

    @pl.when(s == 1)
    def _():
        for g in range(len(news)):
            new_copy(g, True).wait()
            new_copy(g, False).start()

    @pl.when(s == 2)
    def _():
        for g in range(len(news)):
            new_copy(g, False).wait()


COL_CHUNK = 2 * HEAD_DIM


def _finish_head(kind, dil, h, src, w, rope_ref, qkv_ref):
    cols = pl.ds(pl.multiple_of(h * HEAD_DIM, HEAD_DIM), HEAD_DIM)
    per = TILE // dil
    if kind == "qk":
        y = _norm_rope(src[h], w, rope_ref[0], rope_ref[1], rope_ref[2])
        src[h] = y
        if dil == 1:
            qkv_ref[:, cols] = y.astype(BF16)
    elif dil == 1:
        qkv_ref[:, cols] = src[h].astype(BF16)
    if dil > 1:
        for r in range(dil):
            qkv_ref[r * per:(r + 1) * per, cols] = src[h, pl.ds(r, per, stride=dil), :].astype(BF16)


def _proj_prompt_body(*refs, roll, tiles_per_seq, n_rest):
    kinds, shift, nc = roll
    xn_ref, w_ref, rope_ref, qw_ref, kw_ref = refs[:5]
    caches, news = refs[5:5 + nc], refs[5 + nc:5 + 2 * nc]
    qkv_ref, rest_ref = refs[5 + 2 * nc:7 + 2 * nc]
    tails = refs[7 + 2 * nc:7 + 2 * nc + N_GROUPS]
    rolled = refs[7 + 2 * nc + N_GROUPS:7 + 3 * nc + N_GROUPS]
    res, buf, new_buf, sem_in, sem_out, sem_new, sem_tail = refs[7 + 3 * nc + N_GROUPS:]
    i = pl.program_id(0)
    j = pl.program_id(1)
    _roll_step(i * pl.num_programs(1) + j, kinds, shift, caches, news, rolled, buf, new_buf, sem_in, sem_out, sem_new)
    batch = i // tiles_per_seq
    tile_in_seq = i % tiles_per_seq
    slot = j % 2
    heads_per_chunk = COL_CHUNK // HEAD_DIM

    def run(to_res, finish):
        if finish is None:
            z = jnp.dot(xn_ref[...], w_ref[...], preferred_element_type=F32)
            if to_res:
                for h in range(N_HEADS):
                    res[slot, h] = z[:, h * HEAD_DIM:(h + 1) * HEAD_DIM]
            else:
                rest_ref[...] = z
            return
        kind, g = finish
        window, dil = GROUPS[g]
        src = res.at[1 - slot]
        is_q = j == 3 * g + 1
        w = jnp.where(is_q, qw_ref[g:g + 1, :], kw_ref[g:g + 1, :]) if kind == "qk" else None

        n_tail_tiles = -(-window // TILE)
        tail_rows = min(window, TILE)
        first_tail_tile = tiles_per_seq - n_tail_tiles
        writes_tail = tile_in_seq >= first_tail_tile
        if kind == "qk":
            writes_tail = jnp.logical_and(writes_tail, jnp.logical_not(is_q))

        def tail_copy():
            dst_row = (tile_in_seq - first_tail_tile) * tail_rows
            return pltpu.make_async_copy(
                src.at[:, pl.ds(TILE - tail_rows, tail_rows)],
                tails[g].at[batch, 0 if kind == "qk" else 1, :, pl.ds(dst_row, tail_rows)], sem_tail.at[0])

        if kind == "v":
            pl.when(writes_tail)(lambda: tail_copy().start())

        def chunk(c, carry):
            for hh in range(heads_per_chunk):
                _finish_head(kind, dil, c * heads_per_chunk + hh, src, w, rope_ref, qkv_ref)
            col0 = pl.multiple_of(c * COL_CHUNK, COL_CHUNK)
            z = jnp.dot(xn_ref[...], w_ref[:, pl.ds(col0, COL_CHUNK)], preferred_element_type=F32)
            if to_res:
                for hh in range(heads_per_chunk):
                    res[slot, c * heads_per_chunk + hh] = z[:, hh * HEAD_DIM:(hh + 1) * HEAD_DIM]
            else:
                rest_ref[:, pl.ds(col0, COL_CHUNK)] = z
            return carry
        lax.fori_loop(0, TILE // COL_CHUNK, chunk, 0)

        @pl.when(writes_tail)
        def _tail():
            if kind == "qk":
                tail_copy().start()
            tail_copy().wait()

    pl.when(j == 0)(lambda: run(True, None))
    for g in range(N_GROUPS):
        pl.when(jnp.logical_or(j == 3 * g + 1, j == 3 * g + 2))(functools.partial(run, True, ("qk", g)))
        pl.when(j == 3 * g + 3)(functools.partial(run, 3 * g + 3 < N_QKV, ("v", g)))
    pl.when(j > N_QKV)(lambda: run(False, None))


def _proj_prompt(xn, w_bf16, rope, qw, kw, caches, news, batch, seq):
    m, d = xn.shape
    n = w_bf16.shape[1]
    assert m == batch * seq and seq % TILE == 0 and n % TILE == 0 and D_ATT == TILE
    tiles_per_seq = seq // TILE
    nj = n // TILE
    n_rest = nj - N_QKV
    grid = (m // TILE, nj)
    nc = len(caches)
    shift = news[0].shape[1]
    kinds, n_items = _roll_plan(caches, shift)
    assert n_items + ROLL_BUFFERS < grid[0] * grid[1] and all(nw.shape == news[0].shape for nw in news)
    assert all(w % TILE == 0 or TILE % w == 0 for w, _ in GROUPS)

    def w_col(i, j):
        return (0, jnp.where(j < N_QKV, j, jnp.where(j < nj - 1, j + 1, N_QKV)))

    any_spec = pl.BlockSpec(memory_space=pl.ANY)
    in_specs = [
        pl.BlockSpec((TILE, d), lambda i, j: (i, 0)),
        pl.BlockSpec((d, TILE), w_col),
        pl.BlockSpec((3, TILE, HEAD_DIM), lambda i, j: (0, i % tiles_per_seq, 0)),
        pl.BlockSpec((N_GROUPS, HEAD_DIM), lambda i, j: (0, 0)),
        pl.BlockSpec((N_GROUPS, HEAD_DIM), lambda i, j: (0, 0)),
        *([any_spec] * (2 * nc)),
    ]
    out_specs = [
        pl.BlockSpec((None, None, TILE, TILE), lambda i, j: (jnp.clip(j - 1, 0, N_QKV - 1), i, 0, 0)),
        pl.BlockSpec((TILE, TILE), lambda i, j: (i, jnp.clip(j - N_QKV, 0, n_rest - 1))),
        *([any_spec] * (N_GROUPS + nc)),
    ]
    out_shape = [
        jax.ShapeDtypeStruct((N_QKV, m // TILE, TILE, TILE), BF16),
        jax.ShapeDtypeStruct((m, n_rest * TILE), F32),
        *[jax.ShapeDtypeStruct((batch, 2, N_HEADS, w, HEAD_DIM), F32) for w, _ in GROUPS],
        *[jax.ShapeDtypeStruct(c.shape, c.dtype) for c in caches],
    ]
    row_shape = caches[0].shape[2:]
    max_rows = max(k[1] for k in kinds)
    scratch = [
        pltpu.VMEM((2, N_HEADS, TILE, HEAD_DIM), F32),
        pltpu.VMEM((ROLL_BUFFERS, max_rows) + row_shape, F32),
        pltpu.VMEM((nc,) + news[0].shape, F32),
        pltpu.SemaphoreType.DMA((ROLL_BUFFERS,)),
        pltpu.SemaphoreType.DMA((ROLL_BUFFERS,)),
        pltpu.SemaphoreType.DMA((2, nc)),
        pltpu.SemaphoreType.DMA((1,)),
    ]
    row_bytes = 4 * row_shape[0] * row_shape[1]
    est = (2 * TILE * d * 2 + 2 * d * TILE * 2 + 2 * TILE * 3 * HEAD_DIM * 4 + 2 * TILE * TILE * 2 + 2 * TILE * TILE * 4
           + 2 * TILE * TILE * 4 + (ROLL_BUFFERS * max_rows + nc * news[0].shape[0] * shift) * row_bytes + (6 << 20))
    out = pl.pallas_call(
        functools.partial(_proj_prompt_body, roll=(kinds, shift, nc), tiles_per_seq=tiles_per_seq, n_rest=n_rest),
        grid=grid,
        in_specs=in_specs,
        out_specs=out_specs,
        out_shape=out_shape,
        scratch_shapes=scratch,
        compiler_params=pltpu.CompilerParams(
            dimension_semantics=("arbitrary", "arbitrary"),
            vmem_limit_bytes=_vmem_limit(est)),
        name="proj_prompt",
    )(xn, w_bf16, rope, qw, kw, *caches, *news)
    return out[0], out[1], out[2:2 + N_GROUPS], out[2 + N_GROUPS:]


def _subsequence_rows(cur_ref, prev_ref, dil, res, l0, n):
    per = TILE // dil
    pieces = []
    l = l0
    while l < l0 + n:
        ll = l + TILES_PER_SUPER * per if l < 0 else l
        t, off = divmod(ll, per)
        stop = 0 if l < 0 else l0 + n
        take = min(per - off, stop - l)
        if l < 0:
            ref, t = prev_ref, t - (TILES_PER_SUPER - prev_ref.shape[0])
        else:
            ref = cur_ref
        pieces.append(ref[t, res * per + off:res * per + off + take, :])
        l += take
    return pieces[0] if len(pieces) == 1 else jnp.concatenate(pieces, axis=0)


def _attn_prompt_body(*refs):
    cur = refs[:N_QKV]
    prev = refs[N_QKV:N_QKV + 2 * N_GROUPS]
    o_ref, obuf, lbuf = refs[N_QKV + 2 * N_GROUPS:]
    sb = pl.program_id(2)
    qi = lax.broadcasted_iota(jnp.int32, (BAND, 2 * BAND), 0)
    kj = lax.broadcasted_iota(jnp.int32, (BAND, 2 * BAND), 1)
    hi = qi + BAND
    lo_first = jnp.maximum(qi, jnp.where(sb == 0, BAND, 0))
    nt = (((1,), (1,)), ((), ()))
    for g, (_, dil) in enumerate(GROUPS):
        q_ref, k_ref, v_ref = cur[3 * g:3 * g + 3]
        kp_ref, vp_ref = prev[2 * g:2 * g + 2]
        for res in range(dil):
            for blk in range(SUPER // (BAND * dil)):
                l0 = BAND * blk
                q = _subsequence_rows(q_ref, None, dil, res, l0, BAND)
                kc = _subsequence_rows(k_ref, kp_ref, dil, res, l0 - BAND, 2 * BAND)
                vc = _subsequence_rows(v_ref, vp_ref, dil, res, l0 - BAND, 2 * BAND)
                s = lax.dot_general(q, kc, nt, preferred_element_type=F32) * SCALE
                lo = lo_first if blk == 0 else qi
                s = jnp.where(kj >= lo, jnp.where(kj <= hi, s, NEG_INF), NEG_INF)
                m = jnp.max(s, axis=-1, keepdims=True)
                p = jnp.exp(s - m)
                l = jnp.sum(p, axis=-1, keepdims=True)
                o = jnp.dot((p / l).astype(BF16), vc, preferred_element_type=F32)
                start = dil * l0 + res
                rows = pl.ds(start, BAND, stride=dil) if dil > 1 else pl.ds(start, BAND)
                obuf[g, rows, :] = o
                lbuf[g, rows, :] = jnp.broadcast_to(m + jnp.log(l), (BAND, HEAD_DIM))

    rows_per_step = 256

    def combine(i, carry):
        rows = pl.ds(pl.multiple_of(i * rows_per_step, rows_per_step), rows_per_step)
        w0, w1, w2 = _group_weights(lbuf[0, rows, :], lbuf[1, rows, :], lbuf[2, rows, :])
        o_ref[rows, :] = (w0 * obuf[0, rows, :] + w1 * obuf[1, rows, :]) + w2 * obuf[2, rows, :]
        return carry
    lax.fori_loop(0, SUPER // rows_per_step, combine, 0)


def _attn_prompt(qkv, batch, seq):
    assert seq % SUPER == 0 and SUPER % TILE == 0
    nsb = seq // SUPER
    tiles_per_seq = seq // TILE
    in_specs = [
        pl.BlockSpec((None, TILES_PER_SUPER, TILE, HEAD_DIM), lambda b, h, sb, jj=jj: (jj, b * nsb + sb, 0, h))
        for jj in range(N_QKV)
    ]
    args = [qkv] * N_QKV
    prev_bytes = 0
    for g, (_, dil) in enumerate(GROUPS):
        n_prev = -(-BAND * dil // TILE)
        assert TILES_PER_SUPER % n_prev == 0
        per_super = TILES_PER_SUPER // n_prev
        for which in (1, 2):
            in_specs.append(pl.BlockSpec(
                (None, n_prev, TILE, HEAD_DIM),
                lambda b, h, sb, jj=3 * g + which, per_super=per_super:
                    (jj, jnp.maximum((b * nsb + sb) * per_super - 1, b * nsb * per_super), 0, h)))
            args.append(qkv)
            prev_bytes += n_prev * TILE * HEAD_DIM * 2
    est = (2 * (N_QKV * TILES_PER_SUPER * TILE * HEAD_DIM * 2 + prev_bytes + SUPER * HEAD_DIM * 4)
           + 2 * N_GROUPS * SUPER * HEAD_DIM * 4 + (12 << 20))
    return pl.pallas_call(
        _attn_prompt_body,
        grid=(batch, N_HEADS, nsb),
        in_specs=in_specs,
        out_specs=pl.BlockSpec((SUPER, HEAD_DIM), lambda b, h, sb: (b * nsb + sb, h)),
        out_shape=jax.ShapeDtypeStruct((batch * seq, D_ATT), F32),
        scratch_shapes=[
            pltpu.VMEM((N_GROUPS, SUPER, HEAD_DIM), F32),
            pltpu.VMEM((N_GROUPS, SUPER, HEAD_DIM), F32),
        ],
        compiler_params=pltpu.CompilerParams(
            dimension_semantics=("arbitrary", "arbitrary", "arbitrary"),
            vmem_limit_bytes=_vmem_limit(est)),
        name="attn_prompt",
    )(*args)


def _attn_sample_body(qkv_ref, c0_ref, c1_ref, c2_ref, rope_ref, qw_ref, kw_ref,
                      o_ref, n0_ref, n1_ref, n2_ref):
    t_new = qkv_ref.shape[0]
    cache_refs = (c0_ref, c1_ref, c2_ref)
    new_refs = (n0_ref, n1_ref, n2_ref)
    outs = [[None] * N_GROUPS for _ in range(t_new)]
    lses = [[None] * N_GROUPS for _ in range(t_new)]
    for g, (_, dil) in enumerate(GROUPS):
        qs, ks, vs = [], [], []
        for t in range(t_new):
            c, sa, sbn = rope_ref[0, t:t + 1, :], rope_ref[1, t:t + 1, :], rope_ref[2, t:t + 1, :]
            qs.append(_norm_rope(qkv_ref[t, 3 * g], qw_ref[g:g + 1, :], c, sa, sbn))
            ks.append(_norm_rope(qkv_ref[t, 3 * g + 1], kw_ref[g:g + 1, :], c, sa, sbn))
            vs.append(qkv_ref[t, 3 * g + 2])
            new_refs[g][t, 0:N_HEADS, :] = ks[t]
            new_refs[g][t, N_HEADS:2 * N_HEADS, :] = vs[t]
        for t in range(t_new):
            res = t % dil
            first = -((res - t) // dil)
            kc = cache_refs[g][first:, res, 0:N_HEADS, :]
            vc = cache_refs[g][first:, res, N_HEADS:2 * N_HEADS, :]
            q = qs[t]
            s_c = jnp.sum(kc * q[None], axis=-1, keepdims=True) * SCALE
            new_keys = [tp for tp in range(t + 1) if (t - tp) % dil == 0]
            s_n = [jnp.sum(ks[tp] * q, axis=-1, keepdims=True) * SCALE for tp in new_keys]
            m = jnp.max(s_c, axis=0)
            for s in s_n:
                m = jnp.maximum(m, s)
            p_c = jnp.exp(s_c - m[None])
            p_n = [jnp.exp(s - m) for s in s_n]
            l = jnp.sum(p_c, axis=0)
            for p in p_n:
                l = l + p
            o = jnp.sum((p_c / l[None]) * vc, axis=0)
            for p, tp in zip(p_n, new_keys):
                o = o + (p / l) * vs[tp]
            outs[t][g] = o
            lses[t][g] = m + jnp.log(l)
    for t in range(t_new):
        w0, w1, w2 = _group_weights(*lses[t])
        o_ref[t] = (w0 * outs[t][0] + w1 * outs[t][1]) + w2 * outs[t][2]


def _attn_sample(qkv, caches, rope, qw, kw):
    db, t_new = qkv.shape[:2]
    views, cache_specs, cache_bytes = [], [], 0
    for (window, dil), cache in zip(GROUPS, caches):
        assert cache.shape[1] == window == BAND * dil and t_new <= BAND
        n_res = min(dil, t_new)
        views.append(cache.reshape(db, BAND, dil, 2 * N_HEADS, HEAD_DIM))
        cache_specs.append(pl.BlockSpec((None, BAND, n_res, 2 * N_HEADS, HEAD_DIM), lambda b: (b, 0, 0, 0, 0)))
        cache_bytes += BAND * n_res * 2 * N_HEADS * HEAD_DIM * 4
    new_spec = pl.BlockSpec((None, t_new, 2 * N_HEADS, HEAD_DIM), lambda b: (b, 0, 0, 0))
    new_shape = jax.ShapeDtypeStruct((db, t_new, 2 * N_HEADS, HEAD_DIM), F32)
    return pl.pallas_call(
        _attn_sample_body,
        grid=(db,),
        in_specs=[
            pl.BlockSpec((None, t_new, 3 * N_GROUPS, N_HEADS, HEAD_DIM), lambda b: (b, 0, 0, 0, 0)),
            *cache_specs,
            pl.BlockSpec((3, t_new, HEAD_DIM), lambda b: (0, 0, 0)),
            pl.BlockSpec((N_GROUPS, HEAD_DIM), lambda b: (0, 0)),
            pl.BlockSpec((N_GROUPS, HEAD_DIM), lambda b: (0, 0)),
        ],
        out_specs=[pl.BlockSpec((None, t_new, N_HEADS, HEAD_DIM), lambda b: (b, 0, 0, 0)),
                   new_spec, new_spec, new_spec],
        out_shape=[jax.ShapeDtypeStruct((db, t_new, N_HEADS, HEAD_DIM), F32),
                   new_shape, new_shape, new_shape],
        compiler_params=pltpu.CompilerParams(
            dimension_semantics=("arbitrary",), vmem_limit_bytes=_vmem_limit(2 * cache_bytes + (16 << 20))),
        name="attn_sample",
    )(qkv, *views, rope, qw, kw)


def _sigmoid(x):
    return 1.0 / (1.0 + jnp.exp(-x))


def _merge_body(*refs, sample, rows_per_seq):
    if sample:
        (oa_ref, ag_ref, hc_ref, bc_ref, cc_ref, cg_ref, mla_ref, mlc_ref, x_ref, cw_ref,
         wa_ref, wc_ref, wo_ref, p1_ref, p2_ref, y_ref, u_ref, ubuf_ref) = refs
    else:
        (oa_ref, ag_ref, hc_ref, bc_ref, cc_ref, cg_ref, mla_ref, mlc_ref, x_ref, cw_ref,
         wa_ref, wc_ref, wo_ref, y_ref, u_ref, ubuf_ref) = refs
    tm = x_ref.shape[0]

    ag = ag_ref[...]
    ain = (oa_ref[...] * (ag * _sigmoid(ag))).astype(BF16)
    a = jnp.dot(ain, wa_ref[...], preferred_element_type=F32)

    u = cc_ref[...] * hc_ref[...]
    if sample:
        ubuf_ref[0:SUBLANES, :] = jnp.zeros((SUBLANES, u.shape[1]), F32)
    else:
        @pl.when(pl.program_id(0) % (rows_per_seq // tm) == 0)
        def _sequence_start():
            ubuf_ref[0:SUBLANES, :] = jnp.zeros((SUBLANES, u.shape[1]), F32)
    ubuf_ref[SUBLANES:SUBLANES + tm, :] = u
    u1 = ubuf_ref[SUBLANES - 1:SUBLANES - 1 + tm, :]
    u2 = ubuf_ref[SUBLANES - 2:SUBLANES - 2 + tm, :]
    if sample:
        t = lax.broadcasted_iota(jnp.int32, (tm, 1), 0) % rows_per_seq
        u1 = jnp.where(t < 1, p1_ref[...], u1)
        u2 = jnp.where(t < 2, p2_ref[...], u2)
        u_ref[...] = u
    else:
        tail = ubuf_ref[tm:tm + SUBLANES, :]
        ubuf_ref[0:SUBLANES, :] = tail
        u_ref[...] = tail
    conv = (cw_ref[0:1, :] * u2 + cw_ref[1:2, :] * u1) + cw_ref[2:3, :] * u
    cg = cg_ref[...]
    cin = ((bc_ref[...] * conv) * (cg * _sigmoid(cg))).astype(BF16)
    cbr = jnp.dot(cin, wc_ref[...], preferred_element_type=F32)

    m = _sigmoid(mla_ref[...]) * a + _sigmoid(mlc_ref[...]) * cbr
    y_ref[...] = x_ref[...] + jnp.dot(m.astype(BF16), wo_ref[...], preferred_element_type=F32)


def _merge(x, z, o_att, conv_w, wa, wc, wo, *, tm, rows_per_seq, gate_col, conv_col, state=None):
    m_rows, d = x.shape
    sample = state is not None
    assert m_rows % tm == 0 and (sample or rows_per_seq % tm == 0)
    row = lambda i: (i, 0)
    const = lambda i: (0, 0)
    zcol = lambda j: (lambda i: (i, j))
    resident = dict(pipeline_mode=pl.Buffered(1))
    in_specs = [
        pl.BlockSpec((tm, D_ATT), row),
        pl.BlockSpec((tm, D_ATT), zcol(gate_col)),
        pl.BlockSpec((tm, d), zcol(conv_col)),
        pl.BlockSpec((tm, d), zcol(conv_col + 1)),
        pl.BlockSpec((tm, d), zcol(conv_col + 2)),
        pl.BlockSpec((tm, d), zcol(conv_col + 3)),
        pl.BlockSpec((tm, d), zcol(conv_col + 4)),
        pl.BlockSpec((tm, d), zcol(conv_col + 5)),
        pl.BlockSpec((tm, d), row),
        pl.BlockSpec((CONV_WIDTH, d), const),
        pl.BlockSpec((D_ATT, d), const, **resident),
        pl.BlockSpec((d, d), const, **resident),
        pl.BlockSpec((d, d), const, **resident),
    ]
    args = [o_att, z, z, z, z, z, z, z, x, conv_w, wa, wc, wo]
    if sample:
        in_specs += [pl.BlockSpec((tm, d), row), pl.BlockSpec((tm, d), row)]
        args += list(state)
        u_spec = pl.BlockSpec((tm, d), row)
        u_shape = jax.ShapeDtypeStruct((m_rows, d), F32)
    else:
        per_seq = rows_per_seq // tm
        u_spec = pl.BlockSpec((None, SUBLANES, d), lambda i: (i // per_seq, 0, 0))
        u_shape = jax.ShapeDtypeStruct((m_rows // rows_per_seq, SUBLANES, d), F32)
    est = (2 * tm * (2 * D_ATT + 8 * d) * 4 + (D_ATT * d + 2 * d * d) * 2
           + 10 * tm * d * 4 + (4 << 20))
    return pl.pallas_call(
        functools.partial(_merge_body, sample=sample, rows_per_seq=rows_per_seq),
        grid=(m_rows // tm,),
        in_specs=in_specs,
        out_specs=[pl.BlockSpec((tm, d), row), u_spec],
        out_shape=[jax.ShapeDtypeStruct((m_rows, d), F32), u_shape],
        scratch_shapes=[pltpu.VMEM((tm + 2 * SUBLANES, d), F32)],
        compiler_params=pltpu.CompilerParams(
            dimension_semantics=("arbitrary",), vmem_limit_bytes=_vmem_limit(est)),
        name="merge_sample" if sample else "merge_prompt",
    )(*args)


def _layer(xp, xs, caches, state, norm_w, w_in, q_norm_w, k_norm_w, conv_w, w_att_proj, w_conv_proj, w_out):
    b, s, d = xp.shape
    db, t_new, _ = xs.shape
    n = w_in.shape[1]
    assert n == N_QKV * D_ATT + D_ATT + 6 * d and d % D_ATT == 0
    w_in_b = w_in.astype(BF16)
    wa, wc, wo = w_att_proj.astype(BF16), w_conv_proj.astype(BF16), w_out.astype(BF16)
    g = norm_w.reshape(1, d)
    xp2, xs2 = xp.reshape(b * s, d), xs.reshape(db * t_new, d)

    rows_s = db * t_new
    zs = _proj_plain(_rmsnorm(xs2, g, rows_s), w_in_b, TILE)
    rope_s = _rope_tables(PAST_LEN + jnp.arange(t_new, dtype=F32))
    qkv_s = zs[:, :N_QKV * D_ATT].reshape(db, t_new, N_QKV, N_HEADS, HEAD_DIM)
    flat_caches = [c.reshape(db, c.shape[1], 2 * N_HEADS, HEAD_DIM) for c in caches]
    o_att_s, *kv_new = _attn_sample(qkv_s, flat_caches, rope_s, q_norm_w, k_norm_w)
    p1 = jnp.pad(state[:, 1:2], ((0, 0), (0, t_new - 1), (0, 0))).reshape(rows_s, d)
    p2 = jnp.pad(state, ((0, 0), (0, t_new - 2), (0, 0))).reshape(rows_s, d)
    ys, u_s = _merge(xs2, zs, o_att_s.reshape(rows_s, D_ATT), conv_w, wa, wc, wo, tm=rows_s, rows_per_seq=t_new,
                     gate_col=N_QKV, conv_col=(N_QKV * D_ATT + D_ATT) // d, state=(p1, p2))
    conv_s = u_s.reshape(db, t_new, d)[:, t_new - (CONV_WIDTH - 1):]

    rope_p = _rope_tables(jnp.arange(s, dtype=F32))
    qkv_p, rest_p, tails, rolled = _proj_prompt(_rmsnorm(xp2, g, 256), w_in_b, rope_p, q_norm_w, k_norm_w,
                                                flat_caches, kv_new, b, s)
    kv_s = [c.reshape(db, -1, 2, N_HEADS, HEAD_DIM) for c in rolled]
    kv_p = [jnp.transpose(t, (0, 3, 1, 2, 4)) for t in tails]
    o_att_p = _attn_prompt(qkv_p, b, s)
    yp, u_tail = _merge(xp2, rest_p, o_att_p, conv_w, wa, wc, wo, tm=128, rows_per_seq=s,
                        gate_col=6 * d // D_ATT, conv_col=0)
    conv_p = u_tail[:, SUBLANES - (CONV_WIDTH - 1):]

    return yp.reshape(b, s, d), ys.reshape(db, t_new, d), kv_p, conv_p, kv_s, conv_s


def kernel(x_prompt, x_sample, cache_kv_w128, cache_kv_w512, cache_kv_w2048, state_conv, norm_w, w_in, q_norm_w,
           k_norm_w, conv_w, w_att_proj, w_conv_proj, w_out):
    depth = norm_w.shape[0]
    caches = (cache_kv_w128, cache_kv_w512, cache_kv_w2048)
    yp, ys = x_prompt, x_sample
    kv_p = [[] for _ in GROUPS]
    kv_s = [[] for _ in GROUPS]
    conv_p, conv_s = [], []
    for l in range(depth):
        yp, ys, kvp_l, cp_l, kvs_l, cs_l = _layer(
            yp, ys, [c[l] for c in caches], state_conv[l], norm_w[l], w_in[l], q_norm_w[l], k_norm_w[l],
            conv_w[l], w_att_proj[l], w_conv_proj[l], w_out[l])
        for gi in range(N_GROUPS):
            kv_p[gi].append(kvp_l[gi])
            kv_s[gi].append(kvs_l[gi])
        conv_p.append(cp_l)
        conv_s.append(cs_l)
    stack = lambda xs: xs[0][None] if len(xs) == 1 else jnp.stack(xs, axis=0)
    return (yp, ys, stack(kv_p[0]), stack(kv_p[1]), stack(kv_p[2]), stack(conv_p),
            stack(kv_s[0]), stack(kv_s[1]), stack(kv_s[2]), stack(conv_s))
```

```python
import functools

import jax
import jax.numpy as jnp
from jax import lax
from jax.experimental import pallas as pl
from jax.experimental.pallas import tpu as pltpu

F32 = jnp.float32
BF16 = jnp.bfloat16

HEAD_DIM = 128
N_HEADS = 8
GROUPS = ((128, 1), (512, 4), (2048, 16))
N_GROUPS = len(GROUPS)
N_QKV = 3 * N_GROUPS
BAND = 128
D_ATT = N_HEADS * HEAD_DIM
ROT_DIM = HEAD_DIM // 4
ROT_HALF = ROT_DIM // 2
ROPE_THETA = 500000.0
EPS = 1e-6
SCALE = HEAD_DIM ** -0.5
NEG_INF = -1e30
PAST_LEN = 16384
CONV_WIDTH = 3

V7X_VMEM_BYTES = 64 * 1024 * 1024
SUBLANES = 8

TILE = 1024
SUPER = BAND * max(d for _, d in GROUPS)
TILES_PER_SUPER = SUPER // TILE


def _vmem_limit(estimate_bytes):
    return int(min(V7X_VMEM_BYTES - 4 * 1024 * 1024, max(32 * 1024 * 1024, estimate_bytes)))


def _rmsnorm_body(x_ref, g_ref, o_ref):
    x = x_ref[...]
    ms = jnp.mean(x * x, axis=-1, keepdims=True)
    o_ref[...] = ((x * lax.rsqrt(ms + EPS)) * g_ref[...]).astype(BF16)


def _rmsnorm(x, g, tm):
    m, d = x.shape
    assert m % tm == 0
    return pl.pallas_call(
        _rmsnorm_body,
        grid=(m // tm,),
        in_specs=[pl.BlockSpec((tm, d), lambda i: (i, 0)), pl.BlockSpec((1, d), lambda i: (0, 0))],
        out_specs=pl.BlockSpec((tm, d), lambda i: (i, 0)),
        out_shape=jax.ShapeDtypeStruct((m, d), BF16),
        compiler_params=pltpu.CompilerParams(dimension_semantics=("arbitrary",)),
        name="rmsnorm",
    )(x, g)


def _proj_plain_body(x_ref, w_ref, o_ref):
    o_ref[...] = jnp.dot(x_ref[...], w_ref[...], preferred_element_type=F32)


def _proj_plain(xn, w_bf16, tn):
    m, d = xn.shape
    n = w_bf16.shape[1]
    assert n % tn == 0
    return pl.pallas_call(
        _proj_plain_body,
        grid=(n // tn,),
        in_specs=[pl.BlockSpec((m, d), lambda j: (0, 0)), pl.BlockSpec((d, tn), lambda j: (0, j))],
        out_specs=pl.BlockSpec((m, tn), lambda j: (0, j)),
        out_shape=jax.ShapeDtypeStruct((m, n), F32),
        compiler_params=pltpu.CompilerParams(dimension_semantics=("arbitrary",)),
        name="proj_plain",
    )(xn, w_bf16)


def _rope_tables(pos):
    n = pos.shape[0]
    inv = ROPE_THETA ** (-jnp.arange(ROT_HALF, dtype=F32) * (2.0 / ROT_DIM))
    ang = pos[:, None] * inv[None, :]
    cos, sin = jnp.cos(ang), jnp.sin(ang)
    c = jnp.concatenate([cos, cos, jnp.ones((n, HEAD_DIM - ROT_DIM), F32)], axis=1)
    sa = jnp.concatenate([-sin, jnp.zeros((n, HEAD_DIM - ROT_HALF), F32)], axis=1)
    sb = jnp.concatenate([jnp.zeros((n, ROT_HALF), F32), sin, jnp.zeros((n, HEAD_DIM - ROT_DIM), F32)], axis=1)
    return jnp.stack([c, sa, sb], axis=0)


def _norm_rope(x, w, c, sa, sb):
    r = lax.rsqrt(jnp.mean(x * x, axis=-1, keepdims=True) + EPS)
    xn = (x * r) * w
    up = pltpu.roll(xn, HEAD_DIM - ROT_HALF, 1)
    dn = pltpu.roll(xn, ROT_HALF, 1)
    return xn * c + up * sa + dn * sb


def _group_weights(l0, l1, l2):
    mx = jnp.maximum(jnp.maximum(l0, l1), l2)
    e0, e1, e2 = jnp.exp(l0 - mx), jnp.exp(l1 - mx), jnp.exp(l2 - mx)
    den = (e0 + e1) + e2
    return e0 / den, e1 / den, e2 / den


ROLL_ROWS = 300
ROLL_BUFFERS = 3
ROLL_LAG = 1
NEW_ROWS_WAIT = 8


def _roll_plan(caches, shift):
    kinds, first = [], 0
    for g, c in enumerate(caches):
        db, window = c.shape[:2]
        keep = window - shift
        rows = max(r for r in range(1, ROLL_ROWS + 1) if keep % r == 0)
        kinds.append((g, rows, keep // rows, first, db * (keep // rows)))
        first += kinds[-1][4]
    return tuple(kinds), first


def _roll_step(s, kinds, shift, caches, news, outs, buf, sem_in, sem_out, sem_new):
    def copy(kind, t, inbound):
        g, rows, per_batch, first, _ = kind
        b = (t - first) // per_batch
        c = (t - first) % per_batch
        slot = t % ROLL_BUFFERS
        stage = buf.at[slot, pl.ds(0, rows)]
        if inbound:
            return pltpu.make_async_copy(caches[g].at[b, pl.ds(shift + c * rows, rows)], stage, sem_in.at[slot])
        return pltpu.make_async_copy(stage, outs[g].at[b, pl.ds(c * rows, rows)], sem_out.at[slot])

    def for_item(t, fn):
        for kind in kinds:
            pl.when(jnp.logical_and(t >= kind[3], t < kind[3] + kind[4]))(functools.partial(fn, kind, t))

    def turn_around(kind, t):
        copy(kind, t, True).wait()
        copy(kind, t, False).start()

    for_item(s - ROLL_BUFFERS, lambda kind, t: copy(kind, t, False).wait())
    for_item(s, lambda kind, t: copy(kind, t, True).start())
    for_item(s - ROLL_LAG, turn_around)

    def new_copy(g):
        keep = outs[g].shape[1] - shift
        return pltpu.make_async_copy(news[g], outs[g].at[:, pl.ds(keep, shift)], sem_new.at[g])

    @pl.when(s == 0)
    def _():
        for g in range(len(news)):
            new_copy(g).start()

    @pl.when(s == NEW_ROWS_WAIT)
    def _():
        for g in range(len(news)):
            new_copy(g).wait()


COL_CHUNK = 2 * HEAD_DIM


def _finish_head(kind, dil, h, src, w, rope_ref, qkv_ref):
    cols = pl.ds(pl.multiple_of(h * HEAD_DIM, HEAD_DIM), HEAD_DIM)
    per = TILE // dil
    if kind == "qk":
        y = _norm_rope(src[h], w, rope_ref[0], rope_ref[1], rope_ref[2])
        src[h] = y
        if dil == 1:
            qkv_ref[:, cols] = y.astype(BF16)
    elif dil == 1:
        qkv_ref[:, cols] = src[h].astype(BF16)
    if dil > 1:
        for r in range(dil):
            qkv_ref[r * per:(r + 1) * per, cols] = src[h, pl.ds(r, per, stride=dil), :].astype(BF16)


def _finish_conv(kind, hc, hh, half, first_tile, src, acc, ustage, ucarry, cw_ref, cin_ref, uconv_ref):
    if kind == "conv_c":
        acc[hc] = src[hc]
    elif kind == "conv_h":
        chan = pl.ds(pl.multiple_of(half * TILE + hc * HEAD_DIM, HEAD_DIM), HEAD_DIM)
        stage = ustage.at[hh]
        carry = ucarry.at[half * (TILE // HEAD_DIM) + hc]
        stage[0:SUBLANES, :] = jnp.where(first_tile, 0.0, carry[...])
        stage[SUBLANES:SUBLANES + TILE, :] = acc[hc] * src[hc]
        u = stage[SUBLANES:SUBLANES + TILE, :]
        u1 = stage[SUBLANES - 1:SUBLANES - 1 + TILE, :]
        u2 = stage[SUBLANES - 2:SUBLANES - 2 + TILE, :]
        acc[hc] = (cw_ref[0:1, chan] * u2 + cw_ref[1:2, chan] * u1) + cw_ref[2:3, chan] * u
        last = stage[TILE:TILE + SUBLANES, :]
        carry[...] = last
        uconv_ref[:, chan] = last
    elif kind == "conv_b":
        acc[hc] = src[hc] * acc[hc]
    else:
        gate = src[hc]
        cols = pl.ds(pl.multiple_of(hc * HEAD_DIM, HEAD_DIM), HEAD_DIM)
        cin_ref[:, cols] = (acc[hc] * (gate * _sigmoid(gate))).astype(BF16)


def _proj_prompt_body(*refs, roll, tiles_per_seq):
    kinds, shift, nc = roll
    xn_ref, w_ref, rope_ref, qw_ref, kw_ref, cw_ref = refs[:6]
    caches, news = refs[6:6 + nc], refs[6 + nc:6 + 2 * nc]
    qkv_ref, cin_ref, rest_ref, uconv_ref = refs[6 + 2 * nc:10 + 2 * nc]
    tails = refs[10 + 2 * nc:10 + 2 * nc + N_GROUPS]
    rolled = refs[10 + 2 * nc + N_GROUPS:10 + 3 * nc + N_GROUPS]
    res, acc, ustage, ucarry, buf, sem_in, sem_out, sem_new, sem_tail = refs[10 + 3 * nc + N_GROUPS:]
    i = pl.program_id(0)
    j = pl.program_id(1)
    _roll_step(i * pl.num_programs(1) + j, kinds, shift, caches, news, rolled, buf, sem_in, sem_out, sem_new)
    batch = i // tiles_per_seq
    tile_in_seq = i % tiles_per_seq
    slot = j % 2
    heads_per_chunk = COL_CHUNK // HEAD_DIM
    src = res.at[1 - slot]

    def project_chunks(to_res, finish_chunk):
        def chunk(c, carry):
            for hh in range(heads_per_chunk):
                finish_chunk(c * heads_per_chunk + hh, hh)
            col0 = pl.multiple_of(c * COL_CHUNK, COL_CHUNK)
            z = jnp.dot(xn_ref[...], w_ref[:, pl.ds(col0, COL_CHUNK)], preferred_element_type=F32)
            if to_res:
                for hh in range(heads_per_chunk):
                    res[slot, c * heads_per_chunk + hh] = z[:, hh * HEAD_DIM:(hh + 1) * HEAD_DIM]
            else:
                rest_ref[:, pl.ds(col0, COL_CHUNK)] = z
            return carry
        lax.fori_loop(0, TILE // COL_CHUNK, chunk, 0)

    def run_conv(to_res, kind):
        half = (j - (N_QKV + 1)) // 4
        project_chunks(to_res, lambda hc, hh: _finish_conv(
            kind, hc, hh, half, tile_in_seq == 0, src, acc, ustage, ucarry, cw_ref, cin_ref, uconv_ref))

    def run(to_res, finish):
        if finish is None:
            if to_res:
                project_chunks(True, lambda hc, hh: None)
            else:
                rest_ref[...] = jnp.dot(xn_ref[...], w_ref[...], preferred_element_type=F32)
            return
        kind, g = finish
        window, dil = GROUPS[g]
        is_q = j == 3 * g + 1
        w = jnp.where(is_q, qw_ref[g:g + 1, :], kw_ref[g:g + 1, :]) if kind == "qk" else None

        n_tail_tiles = -(-window // TILE)
        tail_rows = min(window, TILE)
        first_tail_tile = tiles_per_seq - n_tail_tiles
        writes_tail = tile_in_seq >= first_tail_tile
        if kind == "qk":
            writes_tail = jnp.logical_and(writes_tail, jnp.logical_not(is_q))

        def tail_copy():
            dst_row = (tile_in_seq - first_tail_tile) * tail_rows
            return pltpu.make_async_copy(
                src.at[:, pl.ds(TILE - tail_rows, tail_rows)],
                tails[g].at[batch, 0 if kind == "qk" else 1, :, pl.ds(dst_row, tail_rows)], sem_tail.at[0])

        if kind == "v":
            pl.when(writes_tail)(lambda: tail_copy().start())

        project_chunks(to_res, lambda h, hh: _finish_head(kind, dil, h, src, w, rope_ref, qkv_ref))

        @pl.when(writes_tail)
        def _tail():
            if kind == "qk":
                tail_copy().start()
            tail_copy().wait()

    conv_first = N_QKV + 1
    pl.when(j == 0)(lambda: run(True, None))
    for g in range(N_GROUPS):
        pl.when(jnp.logical_or(j == 3 * g + 1, j == 3 * g + 2))(functools.partial(run, True, ("qk", g)))
        pl.when(j == 3 * g + 3)(functools.partial(run, True, ("v", g)))
    for k, kind in enumerate(("conv_c", "conv_h", "conv_b")):
        pl.when(jnp.logical_or(j == conv_first + k, j == conv_first + 4 + k))(functools.partial(run_conv, True, kind))
    pl.when(j == conv_first + 3)(functools.partial(run_conv, True, "conv_gate"))
    pl.when(j == conv_first + 7)(functools.partial(run_conv, False, "conv_gate"))
    pl.when(j > conv_first + 7)(lambda: run(False, None))


def _proj_prompt(xn, w_bf16, rope, qw, kw, conv_w, caches, news, batch, seq):
    m, d = xn.shape
    n = w_bf16.shape[1]
    assert m == batch * seq and seq % TILE == 0 and D_ATT == TILE and d == 2 * TILE
    assert n == N_QKV * D_ATT + D_ATT + 6 * d
    tiles_per_seq = seq // TILE
    nj = n // TILE
    grid = (m // TILE, nj)
    nc = len(caches)
    shift = news[0].shape[1]
    kinds, n_items = _roll_plan(caches, shift)
    assert n_items + ROLL_BUFFERS < grid[0] * grid[1] and all(nw.shape == news[0].shape for nw in news)
    assert all(w % TILE == 0 or TILE % w == 0 for w, _ in GROUPS)
    halves = d // TILE
    gate_tile = N_QKV
    h_tile, b_tile, c_tile, cg_tile = (gate_tile + 1 + k * halves for k in range(4))
    logits_tile = cg_tile + halves
    conv_first = N_QKV
    n_conv = 4 * halves
    n_rest = nj - N_QKV - n_conv

    def w_col(i, j):
        k = j - conv_first
        typ = k % 4
        conv_col = jnp.where(typ == 0, c_tile, jnp.where(typ == 1, h_tile, jnp.where(typ == 2, b_tile, cg_tile))) + k // 4
        tail_col = jnp.where(j < nj - 1, logits_tile + (j - conv_first - n_conv), gate_tile)
        return (0, jnp.where(j < conv_first, j, jnp.where(k < n_conv, conv_col, tail_col)))

    any_spec = pl.BlockSpec(memory_space=pl.ANY)
    in_specs = [
        pl.BlockSpec((TILE, d), lambda i, j: (i, 0)),
        pl.BlockSpec((d, TILE), w_col),
        pl.BlockSpec((3, TILE, HEAD_DIM), lambda i, j: (0, i % tiles_per_seq, 0)),
        pl.BlockSpec((N_GROUPS, HEAD_DIM), lambda i, j: (0, 0)),
        pl.BlockSpec((N_GROUPS, HEAD_DIM), lambda i, j: (0, 0)),
        pl.BlockSpec((CONV_WIDTH, d), lambda i, j: (0, 0)),
        *([any_spec] * (2 * nc)),
    ]
    out_specs = [
        pl.BlockSpec((None, None, TILE, TILE), lambda i, j: (jnp.clip(j - 1, 0, N_QKV - 1), i, 0, 0)),
        pl.BlockSpec((TILE, TILE), lambda i, j: (i, jnp.clip((j - conv_first - 1) // 4, 0, halves - 1))),
        pl.BlockSpec((TILE, TILE), lambda i, j: (i, jnp.clip(j - conv_first - n_conv, 0, n_rest - 1))),
        pl.BlockSpec((None, SUBLANES, d), lambda i, j: (i // tiles_per_seq, 0, 0)),
        *([any_spec] * (N_GROUPS + nc)),
    ]
    out_shape = [
        jax.ShapeDtypeStruct((N_QKV, m // TILE, TILE, TILE), BF16),
        jax.ShapeDtypeStruct((m, d), BF16),
        jax.ShapeDtypeStruct((m, n_rest * TILE), F32),
        jax.ShapeDtypeStruct((batch, SUBLANES, d), F32),
        *[jax.ShapeDtypeStruct((batch, 2, N_HEADS, w, HEAD_DIM), F32) for w, _ in GROUPS],
        *[jax.ShapeDtypeStruct(c.shape, c.dtype) for c in caches],
    ]
    row_shape = caches[0].shape[2:]
    max_rows = max(k[1] for k in kinds)
    chunks = TILE // HEAD_DIM
    scratch = [
        pltpu.VMEM((2, chunks, TILE, HEAD_DIM), F32),
        pltpu.VMEM((chunks, TILE, HEAD_DIM), F32),
        pltpu.VMEM((COL_CHUNK // HEAD_DIM, TILE + 2 * SUBLANES, HEAD_DIM), F32),
        pltpu.VMEM((halves * chunks, SUBLANES, HEAD_DIM), F32),
        pltpu.VMEM((ROLL_BUFFERS, max_rows) + row_shape, F32),
        pltpu.SemaphoreType.DMA((ROLL_BUFFERS,)),
        pltpu.SemaphoreType.DMA((ROLL_BUFFERS,)),
        pltpu.SemaphoreType.DMA((nc,)),
        pltpu.SemaphoreType.DMA((1,)),
    ]
    row_bytes = 4 * row_shape[0] * row_shape[1]
    est = (2 * TILE * d * 2 + 2 * d * TILE * 2 + 2 * 3 * TILE * HEAD_DIM * 4
           + 2 * TILE * TILE * 2 * 2 + 2 * TILE * TILE * 4
           + 3 * TILE * TILE * 4 + ROLL_BUFFERS * max_rows * row_bytes + (8 << 20))
    out = pl.pallas_call(
        functools.partial(_proj_prompt_body, roll=(kinds, shift, nc), tiles_per_seq=tiles_per_seq),
        grid=grid,
        in_specs=in_specs,
        out_specs=out_specs,
        out_shape=out_shape,
        scratch_shapes=scratch,
        compiler_params=pltpu.CompilerParams(
            dimension_semantics=("arbitrary", "arbitrary"),
            vmem_limit_bytes=_vmem_limit(est)),
        name="proj_prompt",
    )(xn, w_bf16, rope, qw, kw, conv_w, *caches, *news)
    return out[0], out[1], out[2], out[3], out[4:4 + N_GROUPS], out[4 + N_GROUPS:]


def _subsequence_rows(cur_ref, prev_ref, dil, res, l0, n):
    per = TILE // dil
    pieces = []
    l = l0
    while l < l0 + n:
        ll = l + TILES_PER_SUPER * per if l < 0 else l
        t, off = divmod(ll, per)
        stop = 0 if l < 0 else l0 + n
        take = min(per - off, stop - l)
        if l < 0:
            ref, t = prev_ref, t - (TILES_PER_SUPER - prev_ref.shape[0])
        else:
            ref = cur_ref
        pieces.append(ref[t, res * per + off:res * per + off + take, :])
        l += take
    return pieces[0] if len(pieces) == 1 else jnp.concatenate(pieces, axis=0)


def _attn_prompt_body(*refs):
    cur = refs[:N_QKV]
    prev = refs[N_QKV:N_QKV + 2 * N_GROUPS]
    o_ref, obuf, lbuf = refs[N_QKV + 2 * N_GROUPS:]
    sb = pl.program_id(2)
    qi = lax.broadcasted_iota(jnp.int32, (BAND, 2 * BAND), 0)
    kj = lax.broadcasted_iota(jnp.int32, (BAND, 2 * BAND), 1)
    hi = qi + BAND
    lo_first = jnp.maximum(qi, jnp.where(sb == 0, BAND, 0))
    nt = (((1,), (1,)), ((), ()))
    for g, (_, dil) in enumerate(GROUPS):
        q_ref, k_ref, v_ref = cur[3 * g:3 * g + 3]
        kp_ref, vp_ref = prev[2 * g:2 * g + 2]
        for res in range(dil):
            for blk in range(SUPER // (BAND * dil)):
                l0 = BAND * blk
                q = _subsequence_rows(q_ref, None, dil, res, l0, BAND)
                kc = _subsequence_rows(k_ref, kp_ref, dil, res, l0 - BAND, 2 * BAND)
                vc = _subsequence_rows(v_ref, vp_ref, dil, res, l0 - BAND, 2 * BAND)
                s = lax.dot_general(q, kc, nt, preferred_element_type=F32) * SCALE
                lo = lo_first if blk == 0 else qi
                s = jnp.where(kj >= lo, jnp.where(kj <= hi, s, NEG_INF), NEG_INF)
                m = jnp.max(s, axis=-1, keepdims=True)
                p = jnp.exp(s - m)
                l = jnp.sum(p, axis=-1, keepdims=True)
                o = jnp.dot((p / l).astype(BF16), vc, preferred_element_type=F32)
                start = dil * l0 + res
                rows = pl.ds(start, BAND, stride=dil) if dil > 1 else pl.ds(start, BAND)
                obuf[g, rows, :] = o
                lbuf[g, rows, :] = jnp.broadcast_to(m + jnp.log(l), (BAND, HEAD_DIM))

    rows_per_step = 256

    def combine(i, carry):
        rows = pl.ds(pl.multiple_of(i * rows_per_step, rows_per_step), rows_per_step)
        w0, w1, w2 = _group_weights(lbuf[0, rows, :], lbuf[1, rows, :], lbuf[2, rows, :])
        o_ref[rows, :] = (w0 * obuf[0, rows, :] + w1 * obuf[1, rows, :]) + w2 * obuf[2, rows, :]
        return carry
    lax.fori_loop(0, SUPER // rows_per_step, combine, 0)


def _attn_prompt(qkv, batch, seq):
    assert seq % SUPER == 0 and SUPER % TILE == 0
    nsb = seq // SUPER
    tiles_per_seq = seq // TILE
    in_specs = [
        pl.BlockSpec((None, TILES_PER_SUPER, TILE, HEAD_DIM), lambda b, h, sb, jj=jj: (jj, b * nsb + sb, 0, h))
        for jj in range(N_QKV)
    ]
    args = [qkv] * N_QKV
    prev_bytes = 0
    for g, (_, dil) in enumerate(GROUPS):
        n_prev = -(-BAND * dil // TILE)
        assert TILES_PER_SUPER % n_prev == 0
        per_super = TILES_PER_SUPER // n_prev
        for which in (1, 2):
            in_specs.append(pl.BlockSpec(
                (None, n_prev, TILE, HEAD_DIM),
                lambda b, h, sb, jj=3 * g + which, per_super=per_super:
                    (jj, jnp.maximum((b * nsb + sb) * per_super - 1, b * nsb * per_super), 0, h)))
            args.append(qkv)
            prev_bytes += n_prev * TILE * HEAD_DIM * 2
    est = (2 * (N_QKV * TILES_PER_SUPER * TILE * HEAD_DIM * 2 + prev_bytes + SUPER * HEAD_DIM * 4)
           + 2 * N_GROUPS * SUPER * HEAD_DIM * 4 + (12 << 20))
    return pl.pallas_call(
        _attn_prompt_body,
        grid=(batch, N_HEADS, nsb),
        in_specs=in_specs,
        out_specs=pl.BlockSpec((SUPER, HEAD_DIM), lambda b, h, sb: (b * nsb + sb, h)),
        out_shape=jax.ShapeDtypeStruct((batch * seq, D_ATT), F32),
        scratch_shapes=[
            pltpu.VMEM((N_GROUPS, SUPER, HEAD_DIM), F32),
            pltpu.VMEM((N_GROUPS, SUPER, HEAD_DIM), F32),
        ],
        compiler_params=pltpu.CompilerParams(
            dimension_semantics=("arbitrary", "arbitrary", "arbitrary"),
            vmem_limit_bytes=_vmem_limit(est)),
        name="attn_prompt",
    )(*args)


def _attn_sample_body(qkv_ref, c0_ref, c1_ref, c2_ref, rope_ref, qw_ref, kw_ref,
                      o_ref, n0_ref, n1_ref, n2_ref):
    t_new = qkv_ref.shape[0]
    cache_refs = (c0_ref, c1_ref, c2_ref)
    new_refs = (n0_ref, n1_ref, n2_ref)
    outs = [[None] * N_GROUPS for _ in range(t_new)]
    lses = [[None] * N_GROUPS for _ in range(t_new)]
    for g, (_, dil) in enumerate(GROUPS):
        qs, ks, vs = [], [], []
        for t in range(t_new):
            c, sa, sbn = rope_ref[0, t:t + 1, :], rope_ref[1, t:t + 1, :], rope_ref[2, t:t + 1, :]
            qs.append(_norm_rope(qkv_ref[t, 3 * g], qw_ref[g:g + 1, :], c, sa, sbn))
            ks.append(_norm_rope(qkv_ref[t, 3 * g + 1], kw_ref[g:g + 1, :], c, sa, sbn))
            vs.append(qkv_ref[t, 3 * g + 2])
            new_refs[g][t, 0:N_HEADS, :] = ks[t]
            new_refs[g][t, N_HEADS:2 * N_HEADS, :] = vs[t]
        for t in range(t_new):
            res = t % dil
            first = -((res - t) // dil)
            kc = cache_refs[g][first:, res, 0:N_HEADS, :]
            vc = cache_refs[g][first:, res, N_HEADS:2 * N_HEADS, :]
            q = qs[t]
            s_c = jnp.sum(kc * q[None], axis=-1, keepdims=True) * SCALE
            new_keys = [tp for tp in range(t + 1) if (t - tp) % dil == 0]
            s_n = [jnp.sum(ks[tp] * q, axis=-1, keepdims=True) * SCALE for tp in new_keys]
            m = jnp.max(s_c, axis=0)
            for s in s_n:
                m = jnp.maximum(m, s)
            p_c = jnp.exp(s_c - m[None])
            p_n = [jnp.exp(s - m) for s in s_n]
            l = jnp.sum(p_c, axis=0)
            for p in p_n:
                l = l + p
            o = jnp.sum((p_c / l[None]) * vc, axis=0)
            for p, tp in zip(p_n, new_keys):
                o = o + (p / l) * vs[tp]
            outs[t][g] = o
            lses[t][g] = m + jnp.log(l)
    for t in range(t_new):
        w0, w1, w2 = _group_weights(*lses[t])
        o_ref[t] = (w0 * outs[t][0] + w1 * outs[t][1]) + w2 * outs[t][2]


def _attn_sample(qkv, caches, rope, qw, kw):
    db, t_new = qkv.shape[:2]
    views, cache_specs, cache_bytes = [], [], 0
    for (window, dil), cache in zip(GROUPS, caches):
        assert cache.shape[1] == window == BAND * dil and t_new <= BAND
        n_res = min(dil, t_new)
        views.append(cache.reshape(db, BAND, dil, 2 * N_HEADS, HEAD_DIM))
        cache_specs.append(pl.BlockSpec((None, BAND, n_res, 2 * N_HEADS, HEAD_DIM), lambda b: (b, 0, 0, 0, 0)))
        cache_bytes += BAND * n_res * 2 * N_HEADS * HEAD_DIM * 4
    new_spec = pl.BlockSpec((None, t_new, 2 * N_HEADS, HEAD_DIM), lambda b: (b, 0, 0, 0))
    new_shape = jax.ShapeDtypeStruct((db, t_new, 2 * N_HEADS, HEAD_DIM), F32)
    return pl.pallas_call(
        _attn_sample_body,
        grid=(db,),
        in_specs=[
            pl.BlockSpec((None, t_new, 3 * N_GROUPS, N_HEADS, HEAD_DIM), lambda b: (b, 0, 0, 0, 0)),
            *cache_specs,
            pl.BlockSpec((3, t_new, HEAD_DIM), lambda b: (0, 0, 0)),
            pl.BlockSpec((N_GROUPS, HEAD_DIM), lambda b: (0, 0)),
            pl.BlockSpec((N_GROUPS, HEAD_DIM), lambda b: (0, 0)),
        ],
        out_specs=[pl.BlockSpec((None, t_new, N_HEADS, HEAD_DIM), lambda b: (b, 0, 0, 0)),
                   new_spec, new_spec, new_spec],
        out_shape=[jax.ShapeDtypeStruct((db, t_new, N_HEADS, HEAD_DIM), F32),
                   new_shape, new_shape, new_shape],
        compiler_params=pltpu.CompilerParams(
            dimension_semantics=("arbitrary",), vmem_limit_bytes=_vmem_limit(2 * cache_bytes + (16 << 20))),
        name="attn_sample",
    )(qkv, *views, rope, qw, kw)


def _sigmoid(x):
    return 1.0 / (1.0 + jnp.exp(-x))


def _merge_body(*refs, sample, rows_per_seq):
    if sample:
        (oa_ref, ag_ref, hc_ref, bc_ref, cc_ref, cg_ref, mla_ref, mlc_ref, x_ref, cw_ref,
         wa_ref, wc_ref, wo_ref, p1_ref, p2_ref, y_ref, u_ref, ubuf_ref) = refs
    else:
        (oa_ref, ag_ref, hc_ref, bc_ref, cc_ref, cg_ref, mla_ref, mlc_ref, x_ref, cw_ref,
         wa_ref, wc_ref, wo_ref, y_ref, u_ref, ubuf_ref) = refs
    tm = x_ref.shape[0]

    ag = ag_ref[...]
    ain = (oa_ref[...] * (ag * _sigmoid(ag))).astype(BF16)
    a = jnp.dot(ain, wa_ref[...], preferred_element_type=F32)

    u = cc_ref[...] * hc_ref[...]
    if sample:
        ubuf_ref[0:SUBLANES, :] = jnp.zeros((SUBLANES, u.shape[1]), F32)
    else:
        @pl.when(pl.program_id(0) % (rows_per_seq // tm) == 0)
        def _sequence_start():
            ubuf_ref[0:SUBLANES, :] = jnp.zeros((SUBLANES, u.shape[1]), F32)
    ubuf_ref[SUBLANES:SUBLANES + tm, :] = u
    u1 = ubuf_ref[SUBLANES - 1:SUBLANES - 1 + tm, :]
    u2 = ubuf_ref[SUBLANES - 2:SUBLANES - 2 + tm, :]
    if sample:
        t = lax.broadcasted_iota(jnp.int32, (tm, 1), 0) % rows_per_seq
        u1 = jnp.where(t < 1, p1_ref[...], u1)
        u2 = jnp.where(t < 2, p2_ref[...], u2)
        u_ref[...] = u
    else:
        tail = ubuf_ref[tm:tm + SUBLANES, :]
        ubuf_ref[0:SUBLANES, :] = tail
        u_ref[...] = tail
    conv = (cw_ref[0:1, :] * u2 + cw_ref[1:2, :] * u1) + cw_ref[2:3, :] * u
    cg = cg_ref[...]
    cin = ((bc_ref[...] * conv) * (cg * _sigmoid(cg))).astype(BF16)
    cbr = jnp.dot(cin, wc_ref[...], preferred_element_type=F32)

    m = _sigmoid(mla_ref[...]) * a + _sigmoid(mlc_ref[...]) * cbr
    y_ref[...] = x_ref[...] + jnp.dot(m.astype(BF16), wo_ref[...], preferred_element_type=F32)


def _merge(x, z, o_att, conv_w, wa, wc, wo, *, tm, rows_per_seq, gate_col, conv_col, state=None):
    m_rows, d = x.shape
    sample = state is not None
    assert m_rows % tm == 0 and (sample or rows_per_seq % tm == 0)
    row = lambda i: (i, 0)
    const = lambda i: (0, 0)
    zcol = lambda j: (lambda i: (i, j))
    resident = dict(pipeline_mode=pl.Buffered(1))
    in_specs = [
        pl.BlockSpec((tm, D_ATT), row),
        pl.BlockSpec((tm, D_ATT), zcol(gate_col)),
        pl.BlockSpec((tm, d), zcol(conv_col)),
        pl.BlockSpec((tm, d), zcol(conv_col + 1)),
        pl.BlockSpec((tm, d), zcol(conv_col + 2)),
        pl.BlockSpec((tm, d), zcol(conv_col + 3)),
        pl.BlockSpec((tm, d), zcol(conv_col + 4)),
        pl.BlockSpec((tm, d), zcol(conv_col + 5)),
        pl.BlockSpec((tm, d), row),
        pl.BlockSpec((CONV_WIDTH, d), const),
        pl.BlockSpec((D_ATT, d), const, **resident),
        pl.BlockSpec((d, d), const, **resident),
        pl.BlockSpec((d, d), const, **resident),
    ]
    args = [o_att, z, z, z, z, z, z, z, x, conv_w, wa, wc, wo]
    if sample:
        in_specs += [pl.BlockSpec((tm, d), row), pl.BlockSpec((tm, d), row)]
        args += list(state)
        u_spec = pl.BlockSpec((tm, d), row)
        u_shape = jax.ShapeDtypeStruct((m_rows, d), F32)
    else:
        per_seq = rows_per_seq // tm
        u_spec = pl.BlockSpec((None, SUBLANES, d), lambda i: (i // per_seq, 0, 0))
        u_shape = jax.ShapeDtypeStruct((m_rows // rows_per_seq, SUBLANES, d), F32)
    est = (2 * tm * (2 * D_ATT + 8 * d) * 4 + (D_ATT * d + 2 * d * d) * 2
           + 10 * tm * d * 4 + (4 << 20))
    return pl.pallas_call(
        functools.partial(_merge_body, sample=sample, rows_per_seq=rows_per_seq),
        grid=(m_rows // tm,),
        in_specs=in_specs,
        out_specs=[pl.BlockSpec((tm, d), row), u_spec],
        out_shape=[jax.ShapeDtypeStruct((m_rows, d), F32), u_shape],
        scratch_shapes=[pltpu.VMEM((tm + 2 * SUBLANES, d), F32)],
        compiler_params=pltpu.CompilerParams(
            dimension_semantics=("arbitrary",), vmem_limit_bytes=_vmem_limit(est)),
        name="merge_sample" if sample else "merge_prompt",
    )(*args)


def _merge_prompt_body(oa_ref, ag_ref, cin_ref, mla_ref, mlc_ref, x_ref, wa_ref, wc_ref, wo_ref, y_ref):
    ag = ag_ref[...]
    ain = (oa_ref[...] * (ag * _sigmoid(ag))).astype(BF16)
    a = jnp.dot(ain, wa_ref[...], preferred_element_type=F32)
    cbr = jnp.dot(cin_ref[...], wc_ref[...], preferred_element_type=F32)
    m = _sigmoid(mla_ref[...]) * a + _sigmoid(mlc_ref[...]) * cbr
    y_ref[...] = x_ref[...] + jnp.dot(m.astype(BF16), wo_ref[...], preferred_element_type=F32)


def _merge_prompt(x, o_att, cin, rest, wa, wc, wo, *, tm):
    m_rows, d = x.shape
    assert m_rows % tm == 0 and rest.shape[1] == 2 * d + D_ATT
    row = lambda i: (i, 0)
    const = lambda i: (0, 0)
    resident = dict(pipeline_mode=pl.Buffered(1))
    est = (2 * tm * (2 * D_ATT * 4 + d * 2 + 4 * d * 4) + (D_ATT * d + 2 * d * d) * 2 + 6 * tm * d * 4 + (4 << 20))
    return pl.pallas_call(
        _merge_prompt_body,
        grid=(m_rows // tm,),
        in_specs=[
            pl.BlockSpec((tm, D_ATT), row),
            pl.BlockSpec((tm, D_ATT), lambda i: (i, 2 * d // D_ATT)),
            pl.BlockSpec((tm, d), row),
            pl.BlockSpec((tm, d), lambda i: (i, 0)),
            pl.BlockSpec((tm, d), lambda i: (i, 1)),
            pl.BlockSpec((tm, d), row),
            pl.BlockSpec((D_ATT, d), const, **resident),
            pl.BlockSpec((d, d), const, **resident),
            pl.BlockSpec((d, d), const, **resident),
        ],
        out_specs=pl.BlockSpec((tm, d), row),
        out_shape=jax.ShapeDtypeStruct((m_rows, d), F32),
        compiler_params=pltpu.CompilerParams(
            dimension_semantics=("arbitrary",), vmem_limit_bytes=_vmem_limit(est)),
        name="merge_prompt",
    )(o_att, rest, cin, rest, rest, x, wa, wc, wo)


def _layer(xp, xs, caches, state, norm_w, w_in, q_norm_w, k_norm_w, conv_w, w_att_proj, w_conv_proj, w_out):
    b, s, d = xp.shape
    db, t_new, _ = xs.shape
    n = w_in.shape[1]
    assert n == N_QKV * D_ATT + D_ATT + 6 * d and d % D_ATT == 0
    w_in_b = w_in.astype(BF16)
    wa, wc, wo = w_att_proj.astype(BF16), w_conv_proj.astype(BF16), w_out.astype(BF16)
    g = norm_w.reshape(1, d)
    xp2, xs2 = xp.reshape(b * s, d), xs.reshape(db * t_new, d)

    rows_s = db * t_new
    zs = _proj_plain(_rmsnorm(xs2, g, rows_s), w_in_b, TILE)
    rope_s = _rope_tables(PAST_LEN + jnp.arange(t_new, dtype=F32))
    qkv_s = zs[:, :N_QKV * D_ATT].reshape(db, t_new, N_QKV, N_HEADS, HEAD_DIM)
    flat_caches = [c.reshape(db, c.shape[1], 2 * N_HEADS, HEAD_DIM) for c in caches]
    o_att_s, *kv_new = _attn_sample(qkv_s, flat_caches, rope_s, q_norm_w, k_norm_w)
    p1 = jnp.pad(state[:, 1:2], ((0, 0), (0, t_new - 1), (0, 0))).reshape(rows_s, d)
    p2 = jnp.pad(state, ((0, 0), (0, t_new - 2), (0, 0))).reshape(rows_s, d)
    ys, u_s = _merge(xs2, zs, o_att_s.reshape(rows_s, D_ATT), conv_w, wa, wc, wo, tm=rows_s, rows_per_seq=t_new,
                     gate_col=N_QKV, conv_col=(N_QKV * D_ATT + D_ATT) // d, state=(p1, p2))
    conv_s = u_s.reshape(db, t_new, d)[:, t_new - (CONV_WIDTH - 1):]

    rope_p = _rope_tables(jnp.arange(s, dtype=F32))
    qkv_p, cin_p, rest_p, u_tail, tails, rolled = _proj_prompt(
        _rmsnorm(xp2, g, 256), w_in_b, rope_p, q_norm_w, k_norm_w, conv_w, flat_caches, kv_new, b, s)
    kv_s = [c.reshape(db, -1, 2, N_HEADS, HEAD_DIM) for c in rolled]
    kv_p = [jnp.transpose(t, (0, 3, 1, 2, 4)) for t in tails]
    o_att_p = _attn_prompt(qkv_p, b, s)
    yp = _merge_prompt(xp2, o_att_p, cin_p, rest_p, wa, wc, wo, tm=256)
    conv_p = u_tail[:, SUBLANES - (CONV_WIDTH - 1):]

    return yp.reshape(b, s, d), ys.reshape(db, t_new, d), kv_p, conv_p, kv_s, conv_s


def kernel(x_prompt, x_sample, cache_kv_w128, cache_kv_w512, cache_kv_w2048, state_conv, norm_w, w_in, q_norm_w,
           k_norm_w, conv_w, w_att_proj, w_conv_proj, w_out):
    depth = norm_w.shape[0]
    caches = (cache_kv_w128, cache_kv_w512, cache_kv_w2048)
    yp, ys = x_prompt, x_sample
    kv_p = [[] for _ in GROUPS]
    kv_s = [[] for _ in GROUPS]
    conv_p, conv_s = [], []
    for l in range(depth):
        yp, ys, kvp_l, cp_l, kvs_l, cs_l = _layer(
            yp, ys, [c[l] for c in caches], state_conv[l], norm_w[l], w_in[l], q_norm_w[l], k_norm_w[l],
            conv_w[l], w_att_proj[l], w_conv_proj[l], w_out[l])
        for gi in range(N_GROUPS):
            kv_p[gi].append(kvp_l[gi])
            kv_s[gi].append(kvs_l[gi])
        conv_p.append(cp_l)
        conv_s.append(cs_l)
    stack = lambda xs: xs[0][None] if len(xs) == 1 else jnp.stack(xs, axis=0)
    return (yp, ys, stack(kv_p[0]), stack(kv_p[1]), stack(kv_p[2]), stack(conv_p),
            stack(kv_s[0]), stack(kv_s[1]), stack(kv_s[2]), stack(conv_s))
```

```python
import functools

import jax
import jax.numpy as jnp
from jax import lax
from jax.experimental import pallas as pl
from jax.experimental.pallas import tpu as pltpu

F32 = jnp.float32
BF16 = jnp.bfloat16

HEAD_DIM = 128
N_HEADS = 8
GROUPS = ((128, 1), (512, 4), (2048, 16))
N_GROUPS = len(GROUPS)
N_QKV = 3 * N_GROUPS
BAND = 128
D_ATT = N_HEADS * HEAD_DIM
ROT_DIM = HEAD_DIM // 4
ROT_HALF = ROT_DIM // 2
ROPE_THETA = 500000.0
EPS = 1e-6
SCALE = HEAD_DIM ** -0.5
NEG_INF = -1e30
PAST_LEN = 16384
CONV_WIDTH = 3

V7X_VMEM_BYTES = 64 * 1024 * 1024
SUBLANES = 8

TILE = 1024
SUPER = BAND * max(d for _, d in GROUPS)
TILES_PER_SUPER = SUPER // TILE


def _vmem_limit(estimate_bytes):
    return int(min(V7X_VMEM_BYTES - 8 * 1024 * 1024, max(32 * 1024 * 1024, estimate_bytes)))


def _rmsnorm_body(x_ref, g_ref, o_ref):
    x = x_ref[...]
    ms = jnp.mean(x * x, axis=-1, keepdims=True)
    o_ref[...] = ((x * lax.rsqrt(ms + EPS)) * g_ref[...]).astype(BF16)


def _rmsnorm(x, g, tm):
    m, d = x.shape
    assert m % tm == 0
    return pl.pallas_call(
        _rmsnorm_body,
        grid=(m // tm,),
        in_specs=[pl.BlockSpec((tm, d), lambda i: (i, 0)), pl.BlockSpec((1, d), lambda i: (0, 0))],
        out_specs=pl.BlockSpec((tm, d), lambda i: (i, 0)),
        out_shape=jax.ShapeDtypeStruct((m, d), BF16),
        compiler_params=pltpu.CompilerParams(dimension_semantics=("arbitrary",)),
        name="rmsnorm",
    )(x, g)


def _proj_plain_body(x_ref, w_ref, o_ref, wb_ref):
    wb = w_ref[...].astype(BF16)
    wb_ref[...] = wb
    o_ref[...] = jnp.dot(x_ref[...], wb, preferred_element_type=F32)


def _proj_plain(xn, w, tn):
    m, d = xn.shape
    n = w.shape[1]
    assert n % tn == 0
    est = 2 * d * tn * (4 + 2) + 2 * m * tn * 4 + m * d * 2 + d * tn * 4 + (4 << 20)
    return pl.pallas_call(
        _proj_plain_body,
        grid=(n // tn,),
        in_specs=[pl.BlockSpec((m, d), lambda j: (0, 0)), pl.BlockSpec((d, tn), lambda j: (0, j))],
        out_specs=[pl.BlockSpec((m, tn), lambda j: (0, j)), pl.BlockSpec((d, tn), lambda j: (0, j))],
        out_shape=[jax.ShapeDtypeStruct((m, n), F32), jax.ShapeDtypeStruct((d, n), BF16)],
        compiler_params=pltpu.CompilerParams(
            dimension_semantics=("arbitrary",), vmem_limit_bytes=_vmem_limit(est)),
        name="proj_plain",
    )(xn, w)


def _rope_tables(pos):
    n = pos.shape[0]
    inv = ROPE_THETA ** (-jnp.arange(ROT_HALF, dtype=F32) * (2.0 / ROT_DIM))
    ang = pos[:, None] * inv[None, :]
    cos, sin = jnp.cos(ang), jnp.sin(ang)
    c = jnp.concatenate([cos, cos, jnp.ones((n, HEAD_DIM - ROT_DIM), F32)], axis=1)
    sa = jnp.concatenate([-sin, jnp.zeros((n, HEAD_DIM - ROT_HALF), F32)], axis=1)
    sb = jnp.concatenate([jnp.zeros((n, ROT_HALF), F32), sin, jnp.zeros((n, HEAD_DIM - ROT_DIM), F32)], axis=1)
    return jnp.stack([c, sa, sb], axis=0)


def _norm_rope(x, w, c, sa, sb):
    r = lax.rsqrt(jnp.mean(x * x, axis=-1, keepdims=True) + EPS)
    xn = (x * r) * w
    up = pltpu.roll(xn, HEAD_DIM - ROT_HALF, 1)
    dn = pltpu.roll(xn, ROT_HALF, 1)
    return xn * c + up * sa + dn * sb


def _group_weights(l0, l1, l2):
    mx = jnp.maximum(jnp.maximum(l0, l1), l2)
    e0, e1, e2 = jnp.exp(l0 - mx), jnp.exp(l1 - mx), jnp.exp(l2 - mx)
    den = (e0 + e1) + e2
    return e0 / den, e1 / den, e2 / den


ROLL_ROWS = 300
ROLL_BUFFERS = 4
ROLL_LAG = 2


def _roll_plan(caches, shift):
    kinds, first = [], 0
    for g, c in enumerate(caches):
        db, window = c.shape[:2]
        keep = window - shift
        rows = max(r for r in range(1, ROLL_ROWS + 1) if keep % r == 0)
        kinds.append((g, rows, keep // rows, first, db * (keep // rows)))
        first += kinds[-1][4]
    return tuple(kinds), first


def _roll_step(s, kinds, shift, caches, news, outs, buf, new_buf, sem_in, sem_out, sem_new):
    def copy(kind, t, inbound):
        g, rows, per_batch, first, _ = kind
        b = (t - first) // per_batch
        c = (t - first) % per_batch
        slot = t % ROLL_BUFFERS
        stage = buf.at[slot, pl.ds(0, rows)]
        if inbound:
            return pltpu.make_async_copy(caches[g].at[b, pl.ds(shift + c * rows, rows)], stage, sem_in.at[slot])
        return pltpu.make_async_copy(stage, outs[g].at[b, pl.ds(c * rows, rows)], sem_out.at[slot])

    def for_item(t, fn):
        for kind in kinds:
            pl.when(jnp.logical_and(t >= kind[3], t < kind[3] + kind[4]))(functools.partial(fn, kind, t))

    def turn_around(kind, t):
        copy(kind, t, True).wait()
        copy(kind, t, False).start()

    for_item(s - ROLL_BUFFERS, lambda kind, t: copy(kind, t, False).wait())
    for_item(s, lambda kind, t: copy(kind, t, True).start())
    for_item(s - ROLL_LAG, turn_around)

    def new_copy(g, inbound):
        keep = outs[g].shape[1] - shift
        if inbound:
            return pltpu.make_async_copy(news[g], new_buf.at[g], sem_new.at[0, g])
        return pltpu.make_async_copy(new_buf.at[g], outs[g].at[:, pl.ds(keep, shift)], sem_new.at[1, g])

    @pl.when(s == 0)
    def _():
        for g in range(len(news)):
            new_copy(g, True).start()

    @pl.when(s == 1)
    def _():
        for g in range(len(news)):
            new_copy(g, True).wait()
            new_copy(g, False).start()

    @pl.when(s == 2)
    def _():
        for g in range(len(news)):
            new_copy(g, False).wait()


COL_CHUNK = 2 * HEAD_DIM


def _finish_head(kind, dil, h, src, w, rope_ref, qkv_ref):
    cols = pl.ds(pl.multiple_of(h * HEAD_DIM, HEAD_DIM), HEAD_DIM)
    per = TILE // dil
    if kind == "qk":
        y = _norm_rope(src[h], w, rope_ref[0], rope_ref[1], rope_ref[2])
        src[h] = y
        if dil == 1:
            qkv_ref[:, cols] = y.astype(BF16)
    elif dil == 1:
        qkv_ref[:, cols] = src[h].astype(BF16)
    if dil > 1:
        for r in range(dil):
            qkv_ref[r * per:(r + 1) * per, cols] = src[h, pl.ds(r, per, stride=dil), :].astype(BF16)


def _proj_prompt_body(*refs, roll, tiles_per_seq, n_rest):
    kinds, shift, nc = roll
    xn_ref, w_ref, rope_ref, qw_ref, kw_ref = refs[:5]
    caches, news = refs[5:5 + nc], refs[5 + nc:5 + 2 * nc]
    qkv_ref, rest_ref = refs[5 + 2 * nc:7 + 2 * nc]
    tails = refs[7 + 2 * nc:7 + 2 * nc + N_GROUPS]
    rolled = refs[7 + 2 * nc + N_GROUPS:7 + 3 * nc + N_GROUPS]
    res, buf, new_buf, sem_in, sem_out, sem_new, sem_tail = refs[7 + 3 * nc + N_GROUPS:]
    i = pl.program_id(0)
    j = pl.program_id(1)
    _roll_step(i * pl.num_programs(1) + j, kinds, shift, caches, news, rolled, buf, new_buf, sem_in, sem_out, sem_new)
    batch = i // tiles_per_seq
    tile_in_seq = i % tiles_per_seq
    slot = j % 2
    heads_per_chunk = COL_CHUNK // HEAD_DIM

    def run(to_res, finish):
        if finish is None:
            z = jnp.dot(xn_ref[...], w_ref[...], preferred_element_type=F32)
            if to_res:
                for h in range(N_HEADS):
                    res[slot, h] = z[:, h * HEAD_DIM:(h + 1) * HEAD_DIM]
            else:
                rest_ref[...] = z
            return
        kind, g = finish
        window, dil = GROUPS[g]
        src = res.at[1 - slot]
        is_q = j == 3 * g + 1
        w = jnp.where(is_q, qw_ref[g:g + 1, :], kw_ref[g:g + 1, :]) if kind == "qk" else None

        n_tail_tiles = -(-window // TILE)
        tail_rows = min(window, TILE)
        first_tail_tile = tiles_per_seq - n_tail_tiles
        writes_tail = tile_in_seq >= first_tail_tile
        if kind == "qk":
            writes_tail = jnp.logical_and(writes_tail, jnp.logical_not(is_q))

        def tail_copy():
            dst_row = (tile_in_seq - first_tail_tile) * tail_rows
            return pltpu.make_async_copy(
                src.at[:, pl.ds(TILE - tail_rows, tail_rows)],
                tails[g].at[batch, 0 if kind == "qk" else 1, :, pl.ds(dst_row, tail_rows)], sem_tail.at[0])

        if kind == "v":
            pl.when(writes_tail)(lambda: tail_copy().start())

        def chunk(c, carry):
            for hh in range(heads_per_chunk):
                _finish_head(kind, dil, c * heads_per_chunk + hh, src, w, rope_ref, qkv_ref)
            col0 = pl.multiple_of(c * COL_CHUNK, COL_CHUNK)
            z = jnp.dot(xn_ref[...], w_ref[:, pl.ds(col0, COL_CHUNK)], preferred_element_type=F32)
            if to_res:
                for hh in range(heads_per_chunk):
                    res[slot, c * heads_per_chunk + hh] = z[:, hh * HEAD_DIM:(hh + 1) * HEAD_DIM]
            else:
                rest_ref[:, pl.ds(col0, COL_CHUNK)] = z
            return carry
        lax.fori_loop(0, TILE // COL_CHUNK, chunk, 0)

        @pl.when(writes_tail)
        def _tail():
            if kind == "qk":
                tail_copy().start()
            tail_copy().wait()

    pl.when(j == 0)(lambda: run(True, None))
    for g in range(N_GROUPS):
        pl.when(jnp.logical_or(j == 3 * g + 1, j == 3 * g + 2))(functools.partial(run, True, ("qk", g)))
        pl.when(j == 3 * g + 3)(functools.partial(run, 3 * g + 3 < N_QKV, ("v", g)))
    pl.when(j > N_QKV)(lambda: run(False, None))


def _proj_prompt(xn, w_bf16, rope, qw, kw, caches, news, batch, seq):
    m, d = xn.shape
    n = w_bf16.shape[1]
    assert m == batch * seq and seq % TILE == 0 and n % TILE == 0 and D_ATT == TILE
    tiles_per_seq = seq // TILE
    nj = n // TILE
    n_rest = nj - N_QKV
    grid = (m // TILE, nj)
    nc = len(caches)
    shift = news[0].shape[1]
    kinds, n_items = _roll_plan(caches, shift)
    assert n_items + ROLL_BUFFERS < grid[0] * grid[1] and all(nw.shape == news[0].shape for nw in news)
    assert all(w % TILE == 0 or TILE % w == 0 for w, _ in GROUPS)

    def w_col(i, j):
        return (0, jnp.where(j < N_QKV, j, jnp.where(j < nj - 1, j + 1, N_QKV)))

    any_spec = pl.BlockSpec(memory_space=pl.ANY)
    in_specs = [
        pl.BlockSpec((TILE, d), lambda i, j: (i, 0)),
        pl.BlockSpec((d, TILE), w_col),
        pl.BlockSpec((3, TILE, HEAD_DIM), lambda i, j: (0, i % tiles_per_seq, 0)),
        pl.BlockSpec((N_GROUPS, HEAD_DIM), lambda i, j: (0, 0)),
        pl.BlockSpec((N_GROUPS, HEAD_DIM), lambda i, j: (0, 0)),
        *([any_spec] * (2 * nc)),
    ]
    out_specs = [
        pl.BlockSpec((None, None, TILE, TILE), lambda i, j: (jnp.clip(j - 1, 0, N_QKV - 1), i, 0, 0)),
        pl.BlockSpec((TILE, TILE), lambda i, j: (i, jnp.clip(j - N_QKV, 0, n_rest - 1))),
        *([any_spec] * (N_GROUPS + nc)),
    ]
    out_shape = [
        jax.ShapeDtypeStruct((N_QKV, m // TILE, TILE, TILE), BF16),
        jax.ShapeDtypeStruct((m, n_rest * TILE), F32),
        *[jax.ShapeDtypeStruct((batch, 2, N_HEADS, w, HEAD_DIM), F32) for w, _ in GROUPS],
        *[jax.ShapeDtypeStruct(c.shape, c.dtype) for c in caches],
    ]
    row_shape = caches[0].shape[2:]
    max_rows = max(k[1] for k in kinds)
    scratch = [
        pltpu.VMEM((2, N_HEADS, TILE, HEAD_DIM), F32),
        pltpu.VMEM((ROLL_BUFFERS, max_rows) + row_shape, F32),
        pltpu.VMEM((nc,) + news[0].shape, F32),
        pltpu.SemaphoreType.DMA((ROLL_BUFFERS,)),
        pltpu.SemaphoreType.DMA((ROLL_BUFFERS,)),
        pltpu.SemaphoreType.DMA((2, nc)),
        pltpu.SemaphoreType.DMA((1,)),
    ]
    row_bytes = 4 * row_shape[0] * row_shape[1]
    est = (2 * TILE * d * 2 + 2 * d * TILE * 2 + 2 * TILE * 3 * HEAD_DIM * 4 + 2 * TILE * TILE * 2 + 2 * TILE * TILE * 4
           + 2 * TILE * TILE * 4 + (ROLL_BUFFERS * max_rows + nc * news[0].shape[0] * shift) * row_bytes + (6 << 20))
    out = pl.pallas_call(
        functools.partial(_proj_prompt_body, roll=(kinds, shift, nc), tiles_per_seq=tiles_per_seq, n_rest=n_rest),
        grid=grid,
        in_specs=in_specs,
        out_specs=out_specs,
        out_shape=out_shape,
        scratch_shapes=scratch,
        compiler_params=pltpu.CompilerParams(
            dimension_semantics=("arbitrary", "arbitrary"),
            vmem_limit_bytes=_vmem_limit(est)),
        name="proj_prompt",
    )(xn, w_bf16, rope, qw, kw, *caches, *news)
    return out[0], out[1], out[2:2 + N_GROUPS], out[2 + N_GROUPS:]


def _subsequence_rows(cur_ref, prev_ref, dil, res, l0, n):
    per = TILE // dil
    pieces = []
    l = l0
    while l < l0 + n:
        ll = l + TILES_PER_SUPER * per if l < 0 else l
        t, off = divmod(ll, per)
        stop = 0 if l < 0 else l0 + n
        take = min(per - off, stop - l)
        if l < 0:
            ref, t = prev_ref, t - (TILES_PER_SUPER - prev_ref.shape[0])
        else:
            ref = cur_ref
        pieces.append(ref[t, res * per + off:res * per + off + take, :])
        l += take
    return pieces[0] if len(pieces) == 1 else jnp.concatenate(pieces, axis=0)


def _attn_prompt_body(*refs):
    cur = refs[:N_QKV]
    prev = refs[N_QKV:N_QKV + 2 * N_GROUPS]
    o_ref, obuf, lbuf = refs[N_QKV + 2 * N_GROUPS:]
    sb = pl.program_id(2)
    qi = lax.broadcasted_iota(jnp.int32, (BAND, 2 * BAND), 0)
    kj = lax.broadcasted_iota(jnp.int32, (BAND, 2 * BAND), 1)
    hi = qi + BAND
    lo_first = jnp.maximum(qi, jnp.where(sb == 0, BAND, 0))
    nt = (((1,), (1,)), ((), ()))
    for g, (_, dil) in enumerate(GROUPS):
        q_ref, k_ref, v_ref = cur[3 * g:3 * g + 3]
        kp_ref, vp_ref = prev[2 * g:2 * g + 2]
        for res in range(dil):
            for blk in range(SUPER // (BAND * dil)):
                l0 = BAND * blk
                q = _subsequence_rows(q_ref, None, dil, res, l0, BAND)
                kc = _subsequence_rows(k_ref, kp_ref, dil, res, l0 - BAND, 2 * BAND)
                vc = _subsequence_rows(v_ref, vp_ref, dil, res, l0 - BAND, 2 * BAND)
                s = lax.dot_general(q, kc, nt, preferred_element_type=F32) * SCALE
                lo = lo_first if blk == 0 else qi
                s = jnp.where(kj >= lo, jnp.where(kj <= hi, s, NEG_INF), NEG_INF)
                m = jnp.max(s, axis=-1, keepdims=True)
                p = jnp.exp(s - m)
                l = jnp.sum(p, axis=-1, keepdims=True)
                o = jnp.dot((p / l).astype(BF16), vc, preferred_element_type=F32)
                start = dil * l0 + res
                rows = pl.ds(start, BAND, stride=dil) if dil > 1 else pl.ds(start, BAND)
                obuf[g, rows, :] = o
                lbuf[g, rows, :] = jnp.broadcast_to(m + jnp.log(l), (BAND, HEAD_DIM))

    rows_per_step = 256

    def combine(i, carry):
        rows = pl.ds(pl.multiple_of(i * rows_per_step, rows_per_step), rows_per_step)
        w0, w1, w2 = _group_weights(lbuf[0, rows, :], lbuf[1, rows, :], lbuf[2, rows, :])
        o_ref[rows, :] = (w0 * obuf[0, rows, :] + w1 * obuf[1, rows, :]) + w2 * obuf[2, rows, :]
        return carry
    lax.fori_loop(0, SUPER // rows_per_step, combine, 0)


def _attn_prompt(qkv, batch, seq):
    assert seq % SUPER == 0 and SUPER % TILE == 0
    nsb = seq // SUPER
    in_specs = [
        pl.BlockSpec((None, TILES_PER_SUPER, TILE, HEAD_DIM), lambda b, h, sb, jj=jj: (jj, b * nsb + sb, 0, h))
        for jj in range(N_QKV)
    ]
    args = [qkv] * N_QKV
    prev_bytes = 0
    for g, (_, dil) in enumerate(GROUPS):
        n_prev = -(-BAND * dil // TILE)
        assert TILES_PER_SUPER % n_prev == 0
        per_super = TILES_PER_SUPER // n_prev
        for which in (1, 2):
            in_specs.append(pl.BlockSpec(
                (None, n_prev, TILE, HEAD_DIM),
                lambda b, h, sb, jj=3 * g + which, per_super=per_super:
                    (jj, jnp.maximum((b * nsb + sb) * per_super - 1, b * nsb * per_super), 0, h)))
            args.append(qkv)
            prev_bytes += n_prev * TILE * HEAD_DIM * 2
    est = (2 * (N_QKV * TILES_PER_SUPER * TILE * HEAD_DIM * 2 + prev_bytes + SUPER * HEAD_DIM * 4)
           + 2 * N_GROUPS * SUPER * HEAD_DIM * 4 + (12 << 20))
    return pl.pallas_call(
        _attn_prompt_body,
        grid=(batch, N_HEADS, nsb),
        in_specs=in_specs,
        out_specs=pl.BlockSpec((SUPER, HEAD_DIM), lambda b, h, sb: (b * nsb + sb, h)),
        out_shape=jax.ShapeDtypeStruct((batch * seq, D_ATT), F32),
        scratch_shapes=[
            pltpu.VMEM((N_GROUPS, SUPER, HEAD_DIM), F32),
            pltpu.VMEM((N_GROUPS, SUPER, HEAD_DIM), F32),
        ],
        compiler_params=pltpu.CompilerParams(
            dimension_semantics=("arbitrary", "arbitrary", "arbitrary"),
            vmem_limit_bytes=_vmem_limit(est)),
        name="attn_prompt",
    )(*args)


def _attn_sample_body(qkv_ref, c0_ref, c1_ref, c2_ref, rope_ref, qw_ref, kw_ref,
                      o_ref, n0_ref, n1_ref, n2_ref):
    t_new = qkv_ref.shape[0]
    cache_refs = (c0_ref, c1_ref, c2_ref)
    new_refs = (n0_ref, n1_ref, n2_ref)
    outs = [[None] * N_GROUPS for _ in range(t_new)]
    lses = [[None] * N_GROUPS for _ in range(t_new)]
    for g, (_, dil) in enumerate(GROUPS):
        qs, ks, vs = [], [], []
        for t in range(t_new):
            c, sa, sbn = rope_ref[0, t:t + 1, :], rope_ref[1, t:t + 1, :], rope_ref[2, t:t + 1, :]
            qs.append(_norm_rope(qkv_ref[t, 3 * g], qw_ref[g:g + 1, :], c, sa, sbn))
            ks.append(_norm_rope(qkv_ref[t, 3 * g + 1], kw_ref[g:g + 1, :], c, sa, sbn))
            vs.append(qkv_ref[t, 3 * g + 2])
            new_refs[g][t, 0:N_HEADS, :] = ks[t]
            new_refs[g][t, N_HEADS:2 * N_HEADS, :] = vs[t]
        for t in range(t_new):
            res = t % dil
            first = -((res - t) // dil)
            kc = cache_refs[g][first:, res, 0:N_HEADS, :]
            vc = cache_refs[g][first:, res, N_HEADS:2 * N_HEADS, :]
            q = qs[t]
            s_c = jnp.sum(kc * q[None], axis=-1, keepdims=True) * SCALE
            new_keys = [tp for tp in range(t + 1) if (t - tp) % dil == 0]
            s_n = [jnp.sum(ks[tp] * q, axis=-1, keepdims=True) * SCALE for tp in new_keys]
            m = jnp.max(s_c, axis=0)
            for s in s_n:
                m = jnp.maximum(m, s)
            p_c = jnp.exp(s_c - m[None])
            p_n = [jnp.exp(s - m) for s in s_n]
            l = jnp.sum(p_c, axis=0)
            for p in p_n:
                l = l + p
            o = jnp.sum((p_c / l[None]) * vc, axis=0)
            for p, tp in zip(p_n, new_keys):
                o = o + (p / l) * vs[tp]
            outs[t][g] = o
            lses[t][g] = m + jnp.log(l)
    for t in range(t_new):
        w0, w1, w2 = _group_weights(*lses[t])
        o_ref[t] = (w0 * outs[t][0] + w1 * outs[t][1]) + w2 * outs[t][2]


def _attn_sample(qkv, caches, rope, qw, kw):
    db, t_new = qkv.shape[:2]
    views, cache_specs, cache_bytes = [], [], 0
    for (window, dil), cache in zip(GROUPS, caches):
        assert cache.shape[1] == window == BAND * dil and t_new <= BAND
        n_res = min(dil, t_new)
        views.append(cache.reshape(db, BAND, dil, 2 * N_HEADS, HEAD_DIM))
        cache_specs.append(pl.BlockSpec((None, BAND, n_res, 2 * N_HEADS, HEAD_DIM), lambda b: (b, 0, 0, 0, 0)))
        cache_bytes += BAND * n_res * 2 * N_HEADS * HEAD_DIM * 4
    new_spec = pl.BlockSpec((None, t_new, 2 * N_HEADS, HEAD_DIM), lambda b: (b, 0, 0, 0))
    new_shape = jax.ShapeDtypeStruct((db, t_new, 2 * N_HEADS, HEAD_DIM), F32)
    return pl.pallas_call(
        _attn_sample_body,
        grid=(db,),
        in_specs=[
            pl.BlockSpec((None, t_new, 3 * N_GROUPS, N_HEADS, HEAD_DIM), lambda b: (b, 0, 0, 0, 0)),
            *cache_specs,
            pl.BlockSpec((3, t_new, HEAD_DIM), lambda b: (0, 0, 0)),
            pl.BlockSpec((N_GROUPS, HEAD_DIM), lambda b: (0, 0)),
            pl.BlockSpec((N_GROUPS, HEAD_DIM), lambda b: (0, 0)),
        ],
        out_specs=[pl.BlockSpec((None, t_new, N_HEADS, HEAD_DIM), lambda b: (b, 0, 0, 0)),
                   new_spec, new_spec, new_spec],
        out_shape=[jax.ShapeDtypeStruct((db, t_new, N_HEADS, HEAD_DIM), F32),
                   new_shape, new_shape, new_shape],
        compiler_params=pltpu.CompilerParams(
            dimension_semantics=("arbitrary",), vmem_limit_bytes=_vmem_limit(2 * cache_bytes + (16 << 20))),
        name="attn_sample",
    )(qkv, *views, rope, qw, kw)


def _sigmoid(x):
    return 1.0 / (1.0 + jnp.exp(-x))


def _merge_body(*refs, sample, rows_per_seq):
    if sample:
        (oa_ref, ag_ref, hc_ref, bc_ref, cc_ref, cg_ref, mla_ref, mlc_ref, x_ref, cw_ref,
         wa_ref, wc_ref, wo_ref, p1_ref, p2_ref, y_ref, u_ref, hist_ref) = refs
    else:
        (oa_ref, ag_ref, hc_ref, bc_ref, cc_ref, cg_ref, mla_ref, mlc_ref, x_ref, cw_ref,
         wa_ref, wc_ref, wo_ref, y_ref, u_ref, hist_ref) = refs
    tm = x_ref.shape[0]

    ag = ag_ref[...]
    ain = (oa_ref[...] * (ag * _sigmoid(ag))).astype(BF16)
    a = jnp.dot(ain, wa_ref[...], preferred_element_type=F32)

    u = cc_ref[...] * hc_ref[...]
    if sample:
        hist_ref[...] = jnp.zeros(hist_ref.shape, F32)
    else:
        @pl.when(pl.program_id(0) % (rows_per_seq // tm) == 0)
        def _sequence_start():
            hist_ref[...] = jnp.zeros(hist_ref.shape, F32)
    ext = jnp.concatenate([hist_ref[...], u], axis=0)
    u1 = pltpu.roll(ext, 1, 0)[SUBLANES:, :]
    u2 = pltpu.roll(ext, 2, 0)[SUBLANES:, :]
    if sample:
        t = lax.broadcasted_iota(jnp.int32, (tm, 1), 0) % rows_per_seq
        u1 = jnp.where(t < 1, p1_ref[...], u1)
        u2 = jnp.where(t < 2, p2_ref[...], u2)
        u_ref[...] = u
    else:
        tail = u[tm - SUBLANES:, :]
        hist_ref[...] = tail
        u_ref[...] = tail
    conv = (cw_ref[0:1, :] * u2 + cw_ref[1:2, :] * u1) + cw_ref[2:3, :] * u
    cg = cg_ref[...]
    cin = ((bc_ref[...] * conv) * (cg * _sigmoid(cg))).astype(BF16)
    cbr = jnp.dot(cin, wc_ref[...], preferred_element_type=F32)

    m = _sigmoid(mla_ref[...]) * a + _sigmoid(mlc_ref[...]) * cbr
    y_ref[...] = x_ref[...] + jnp.dot(m.astype(BF16), wo_ref[...], preferred_element_type=F32)


def _merge(x, z, o_att, conv_w, wa, wc, wo, *, tm, rows_per_seq, gate_col, conv_col, state=None):
    m_rows, d = x.shape
    sample = state is not None
    assert m_rows % tm == 0 and (sample or rows_per_seq % tm == 0)
    row = lambda i: (i, 0)
    const = lambda i: (0, 0)
    zcol = lambda j: (lambda i: (i, j))
    resident = dict(pipeline_mode=pl.Buffered(1))
    in_specs = [
        pl.BlockSpec((tm, D_ATT), row),
        pl.BlockSpec((tm, D_ATT), zcol(gate_col)),
        pl.BlockSpec((tm, d), zcol(conv_col)),
        pl.BlockSpec((tm, d), zcol(conv_col + 1)),
        pl.BlockSpec((tm, d), zcol(conv_col + 2)),
        pl.BlockSpec((tm, d), zcol(conv_col + 3)),
        pl.BlockSpec((tm, d), zcol(conv_col + 4)),
        pl.BlockSpec((tm, d), zcol(conv_col + 5)),
        pl.BlockSpec((tm, d), row),
        pl.BlockSpec((CONV_WIDTH, d), const),
        pl.BlockSpec((D_ATT, d), const, **resident),
        pl.BlockSpec((d, d), const, **resident),
        pl.BlockSpec((d, d), const, **resident),
    ]
    args = [o_att, z, z, z, z, z, z, z, x, conv_w, wa, wc, wo]
    if sample:
        in_specs += [pl.BlockSpec((tm, d), row), pl.BlockSpec((tm, d), row)]
        args += list(state)
        u_spec = pl.BlockSpec((tm, d), row)
        u_shape = jax.ShapeDtypeStruct((m_rows, d), F32)
    else:
        per_seq = rows_per_seq // tm
        u_spec = pl.BlockSpec((None, SUBLANES, d), lambda i: (i // per_seq, 0, 0))
        u_shape = jax.ShapeDtypeStruct((m_rows // rows_per_seq, SUBLANES, d), F32)
    est = (2 * tm * (2 * D_ATT + 8 * d) * 4 + (D_ATT * d + 2 * d * d) * 2
           + 10 * tm * d * 4 + (4 << 20))
    return pl.pallas_call(
        functools.partial(_merge_body, sample=sample, rows_per_seq=rows_per_seq),
        grid=(m_rows // tm,),
        in_specs=in_specs,
        out_specs=[pl.BlockSpec((tm, d), row), u_spec],
        out_shape=[jax.ShapeDtypeStruct((m_rows, d), F32), u_shape],
        scratch_shapes=[pltpu.VMEM((SUBLANES, d), F32)],
        compiler_params=pltpu.CompilerParams(
            dimension_semantics=("arbitrary",), vmem_limit_bytes=_vmem_limit(est)),
        name="merge_sample" if sample else "merge_prompt",
    )(*args)


def _layer(xp, xs, caches, state, norm_w, w_in, q_norm_w, k_norm_w, conv_w, w_att_proj, w_conv_proj, w_out):
    b, s, d = xp.shape
    db, t_new, _ = xs.shape
    n = w_in.shape[1]
    assert n == N_QKV * D_ATT + D_ATT + 6 * d and d % D_ATT == 0
    wa, wc, wo = w_att_proj.astype(BF16), w_conv_proj.astype(BF16), w_out.astype(BF16)
    g = norm_w.reshape(1, d)
    xp2, xs2 = xp.reshape(b * s, d), xs.reshape(db * t_new, d)

    rows_s = db * t_new
    zs, w_in_b = _proj_plain(_rmsnorm(xs2, g, rows_s), w_in, TILE)
    rope_s = _rope_tables(PAST_LEN + jnp.arange(t_new, dtype=F32))
    qkv_s = zs[:, :N_QKV * D_ATT].reshape(db, t_new, N_QKV, N_HEADS, HEAD_DIM)
    flat_caches = [c.reshape(db, c.shape[1], 2 * N_HEADS, HEAD_DIM) for c in caches]
    o_att_s, *kv_new = _attn_sample(qkv_s, flat_caches, rope_s, q_norm_w, k_norm_w)
    p1 = jnp.pad(state[:, 1:2], ((0, 0), (0, t_new - 1), (0, 0))).reshape(rows_s, d)
    p2 = jnp.pad(state, ((0, 0), (0, t_new - 2), (0, 0))).reshape(rows_s, d)
    ys, u_s = _merge(xs2, zs, o_att_s.reshape(rows_s, D_ATT), conv_w, wa, wc, wo, tm=rows_s, rows_per_seq=t_new,
                     gate_col=N_QKV, conv_col=(N_QKV * D_ATT + D_ATT) // d, state=(p1, p2))
    conv_s = u_s.reshape(db, t_new, d)[:, t_new - (CONV_WIDTH - 1):]

    rope_p = _rope_tables(jnp.arange(s, dtype=F32))
    qkv_p, rest_p, tails, rolled = _proj_prompt(_rmsnorm(xp2, g, 256), w_in_b, rope_p, q_norm_w, k_norm_w,
                                                flat_caches, kv_new, b, s)
    kv_s = [c.reshape(db, -1, 2, N_HEADS, HEAD_DIM) for c in rolled]
    kv_p = [jnp.transpose(t, (0, 3, 1, 2, 4)) for t in tails]
    o_att_p = _attn_prompt(qkv_p, b, s)
    yp, u_tail = _merge(xp2, rest_p, o_att_p, conv_w, wa, wc, wo, tm=128, rows_per_seq=s,
                        gate_col=6 * d // D_ATT, conv_col=0)
    conv_p = u_tail[:, SUBLANES - (CONV_WIDTH - 1):]

    return yp.reshape(b, s, d), ys.reshape(db, t_new, d), kv_p, conv_p, kv_s, conv_s


def kernel(x_prompt, x_sample, cache_kv_w128, cache_kv_w512, cache_kv_w2048, state_conv, norm_w, w_in, q_norm_w,
           k_norm_w, conv_w, w_att_proj, w_conv_proj, w_out):
    depth = norm_w.shape[0]
    caches = (cache_kv_w128, cache_kv_w512, cache_kv_w2048)
    yp, ys = x_prompt, x_sample
    kv_p = [[] for _ in GROUPS]
    kv_s = [[] for _ in GROUPS]
    conv_p, conv_s = [], []
    for l in range(depth):
        yp, ys, kvp_l, cp_l, kvs_l, cs_l = _layer(
            yp, ys, [c[l] for c in caches], state_conv[l], norm_w[l], w_in[l], q_norm_w[l], k_norm_w[l],
            conv_w[l], w_att_proj[l], w_conv_proj[l], w_out[l])
        for gi in range(N_GROUPS):
            kv_p[gi].append(kvp_l[gi])
            kv_s[gi].append(kvs_l[gi])
        conv_p.append(cp_l)
        conv_s.append(cs_l)
    stack = lambda xs: xs[0][None] if len(xs) == 1 else jnp.stack(xs, axis=0)
    return (yp, ys, stack(kv_p[0]), stack(kv_p[1]), stack(kv_p[2]), stack(conv_p),
            stack(kv_s[0]), stack(kv_s[1]), stack(kv_s[2]), stack(conv_s))
```

```python
import functools

import jax
import jax.numpy as jnp
from jax import lax
from jax.experimental import pallas as pl
from jax.experimental.pallas import tpu as pltpu

F32 = jnp.float32
BF16 = jnp.bfloat16

HEAD_DIM = 128
N_HEADS = 8
GROUPS = ((128, 1), (512, 4), (2048, 16))
N_GROUPS = len(GROUPS)
N_QKV = 3 * N_GROUPS
BAND = 128
D_ATT = N_HEADS * HEAD_DIM
ROT_DIM = HEAD_DIM // 4
ROT_HALF = ROT_DIM // 2
ROPE_THETA = 500000.0
EPS = 1e-6
SCALE = HEAD_DIM ** -0.5
NEG_INF = -1e30
PAST_LEN = 16384
CONV_WIDTH = 3

V7X_VMEM_BYTES = 64 * 1024 * 1024
SUBLANES = 8

TILE = 1024
SUPER = BAND * max(d for _, d in GROUPS)
TILES_PER_SUPER = SUPER // TILE


def _vmem_limit(estimate_bytes):
    return int(min(V7X_VMEM_BYTES - 8 * 1024 * 1024, max(32 * 1024 * 1024, estimate_bytes)))


def _rmsnorm_body(x_ref, g_ref, o_ref):
    x = x_ref[...]
    ms = jnp.mean(x * x, axis=-1, keepdims=True)
    o_ref[...] = ((x * lax.rsqrt(ms + EPS)) * g_ref[...]).astype(BF16)


def _rmsnorm(x, g, tm):
    m, d = x.shape
    assert m % tm == 0
    return pl.pallas_call(
        _rmsnorm_body,
        grid=(m // tm,),
        in_specs=[pl.BlockSpec((tm, d), lambda i: (i, 0)), pl.BlockSpec((1, d), lambda i: (0, 0))],
        out_specs=pl.BlockSpec((tm, d), lambda i: (i, 0)),
        out_shape=jax.ShapeDtypeStruct((m, d), BF16),
        compiler_params=pltpu.CompilerParams(
            dimension_semantics=("arbitrary",), vmem_limit_bytes=_vmem_limit(2 * tm * d * (4 + 2) + 3 * tm * d * 4)),
        name="rmsnorm",
    )(x, g)


def _proj_plain_body(x_ref, w_ref, o_ref, wb_ref):
    wb = w_ref[...].astype(BF16)
    wb_ref[...] = wb
    o_ref[...] = jnp.dot(x_ref[...], wb, preferred_element_type=F32)


def _proj_plain(xn, w, tn):
    m, d = xn.shape
    n = w.shape[1]
    assert n % tn == 0
    est = 2 * d * tn * (4 + 2) + 2 * m * tn * 4 + m * d * 2 + d * tn * 4 + (4 << 20)
    return pl.pallas_call(
        _proj_plain_body,
        grid=(n // tn,),
        in_specs=[pl.BlockSpec((m, d), lambda j: (0, 0)), pl.BlockSpec((d, tn), lambda j: (0, j))],
        out_specs=[pl.BlockSpec((m, tn), lambda j: (0, j)), pl.BlockSpec((d, tn), lambda j: (0, j))],
        out_shape=[jax.ShapeDtypeStruct((m, n), F32), jax.ShapeDtypeStruct((d, n), BF16)],
        compiler_params=pltpu.CompilerParams(
            dimension_semantics=("arbitrary",), vmem_limit_bytes=_vmem_limit(est)),
        name="proj_plain",
    )(xn, w)


def _rope_tables(pos):
    n = pos.shape[0]
    inv = ROPE_THETA ** (-jnp.arange(ROT_HALF, dtype=F32) * (2.0 / ROT_DIM))
    ang = pos[:, None] * inv[None, :]
    cos, sin = jnp.cos(ang), jnp.sin(ang)
    c = jnp.concatenate([cos, cos, jnp.ones((n, HEAD_DIM - ROT_DIM), F32)], axis=1)
    sa = jnp.concatenate([-sin, jnp.zeros((n, HEAD_DIM - ROT_HALF), F32)], axis=1)
    sb = jnp.concatenate([jnp.zeros((n, ROT_HALF), F32), sin, jnp.zeros((n, HEAD_DIM - ROT_DIM), F32)], axis=1)
    return jnp.stack([c, sa, sb], axis=0)


def _norm_rope(x, w, c, sa, sb):
    r = lax.rsqrt(jnp.mean(x * x, axis=-1, keepdims=True) + EPS)
    xn = (x * r) * w
    up = pltpu.roll(xn, HEAD_DIM - ROT_HALF, 1)
    dn = pltpu.roll(xn, ROT_HALF, 1)
    return xn * c + up * sa + dn * sb


def _group_weights(l0, l1, l2):
    mx = jnp.maximum(jnp.maximum(l0, l1), l2)
    e0, e1, e2 = jnp.exp(l0 - mx), jnp.exp(l1 - mx), jnp.exp(l2 - mx)
    den = (e0 + e1) + e2
    return e0 / den, e1 / den, e2 / den


ROLL_ROWS = 300
ROLL_BUFFERS = 4
ROLL_LAG = 2


def _roll_plan(caches, shift):
    kinds, first = [], 0
    for g, c in enumerate(caches):
        db, window = c.shape[:2]
        keep = window - shift
        rows = max(r for r in range(1, ROLL_ROWS + 1) if keep % r == 0)
        kinds.append((g, rows, keep // rows, first, db * (keep // rows)))
        first += kinds[-1][4]
    return tuple(kinds), first


def _roll_step(s, kinds, shift, caches, news, outs, buf, new_buf, sem_in, sem_out, sem_new):
    def copy(kind, t, inbound):
        g, rows, per_batch, first, _ = kind
        b = (t - first) // per_batch
        c = (t - first) % per_batch
        slot = t % ROLL_BUFFERS
        stage = buf.at[slot, pl.ds(0, rows)]
        if inbound:
            return pltpu.make_async_copy(caches[g].at[b, pl.ds(shift + c * rows, rows)], stage, sem_in.at[slot])
        return pltpu.make_async_copy(stage, outs[g].at[b, pl.ds(c * rows, rows)], sem_out.at[slot])

    def for_item(t, fn):
        for kind in kinds:
            pl.when(jnp.logical_and(t >= kind[3], t < kind[3] + kind[4]))(functools.partial(fn, kind, t))

    def turn_around(kind, t):
        copy(kind, t, True).wait()
        copy(kind, t, False).start()

    for_item(s - ROLL_BUFFERS, lambda kind, t: copy(kind, t, False).wait())
    for_item(s, lambda kind, t: copy(kind, t, True).start())
    for_item(s - ROLL_LAG, turn_around)

    def new_copy(g, inbound):
        keep = outs[g].shape[1] - shift
        if inbound:
            return pltpu.make_async_copy(news[g], new_buf.at[g], sem_new.at[0, g])
        return pltpu.make_async_copy(new_buf.at[g], outs[g].at[:, pl.ds(keep, shift)], sem_new.at[1, g])

    @pl.when(s == 0)
    def _():
        for g in range(len(news)):
            new_copy(g, True).start()

    @pl.when(s == 1)
    def _():
        for g in range(len(news)):
            new_copy(g, True).wait()
            new_copy(g, False).start()

    @pl.when(s == 2)
    def _():
        for g in range(len(news)):
            new_copy(g, False).wait()


COL_CHUNK = 2 * HEAD_DIM


REGROUP_STRIDE = 4


def _finish_head(kind, dil, h, src, w, rope_ref, qkv_ref, stage):
    cols = pl.ds(pl.multiple_of(h * HEAD_DIM, HEAD_DIM), HEAD_DIM)
    per = TILE // dil
    if kind == "qk":
        y = _norm_rope(src[h], w, rope_ref[0], rope_ref[1], rope_ref[2])
        src[h] = y
        if dil == 1:
            qkv_ref[:, cols] = y.astype(BF16)
    elif dil == 1:
        qkv_ref[:, cols] = src[h].astype(BF16)
    if 1 < dil <= REGROUP_STRIDE:
        for r in range(dil):
            qkv_ref[r * per:(r + 1) * per, cols] = src[h, pl.ds(r, per, stride=dil), :].astype(BF16)
    elif dil > REGROUP_STRIDE:
        assert dil % REGROUP_STRIDE == 0 and dil // REGROUP_STRIDE <= REGROUP_STRIDE
        coarse = TILE // REGROUP_STRIDE
        hi_stride = dil // REGROUP_STRIDE
        for r_lo in range(REGROUP_STRIDE):
            stage[r_lo * coarse:(r_lo + 1) * coarse, :] = src[h, pl.ds(r_lo, coarse, stride=REGROUP_STRIDE), :]
        for r_lo in range(REGROUP_STRIDE):
            for r_hi in range(hi_stride):
                r = REGROUP_STRIDE * r_hi + r_lo
                qkv_ref[r * per:(r + 1) * per, cols] = (
                    stage[pl.ds(r_lo * coarse + r_hi, per, stride=hi_stride), :].astype(BF16))


def _proj_prompt_body(*refs, roll, tiles_per_seq, n_rest):
    kinds, shift, nc = roll
    xn_ref, w_ref, rope_ref, qw_ref, kw_ref = refs[:5]
    caches, news = refs[5:5 + nc], refs[5 + nc:5 + 2 * nc]
    qkv_ref, rest_ref = refs[5 + 2 * nc:7 + 2 * nc]
    tails = refs[7 + 2 * nc:7 + 2 * nc + N_GROUPS]
    rolled = refs[7 + 2 * nc + N_GROUPS:7 + 3 * nc + N_GROUPS]
    res, stage, buf, new_buf, sem_in, sem_out, sem_new, sem_tail = refs[7 + 3 * nc + N_GROUPS:]
    i = pl.program_id(0)
    j = pl.program_id(1)
    _roll_step(i * pl.num_programs(1) + j, kinds, shift, caches, news, rolled, buf, new_buf, sem_in, sem_out, sem_new)
    batch = i // tiles_per_seq
    tile_in_seq = i % tiles_per_seq
    slot = j % 2
    heads_per_chunk = COL_CHUNK // HEAD_DIM

    def run(to_res, finish):
        if finish is None:
            z = jnp.dot(xn_ref[...], w_ref[...], preferred_element_type=F32)
            if to_res:
                for h in range(N_HEADS):
                    res[slot, h] = z[:, h * HEAD_DIM:(h + 1) * HEAD_DIM]
            else:
                rest_ref[...] = z
            return
        kind, g = finish
        window, dil = GROUPS[g]
        src = res.at[1 - slot]
        is_q = j == 3 * g + 1
        w = jnp.where(is_q, qw_ref[g:g + 1, :], kw_ref[g:g + 1, :]) if kind == "qk" else None

        n_tail_tiles = -(-window // TILE)
        tail_rows = min(window, TILE)
        first_tail_tile = tiles_per_seq - n_tail_tiles
        writes_tail = tile_in_seq >= first_tail_tile
        if kind == "qk":
            writes_tail = jnp.logical_and(writes_tail, jnp.logical_not(is_q))

        def tail_copy():
            dst_row = (tile_in_seq - first_tail_tile) * tail_rows
            return pltpu.make_async_copy(
                src.at[:, pl.ds(TILE - tail_rows, tail_rows)],
                tails[g].at[batch, 0 if kind == "qk" else 1, :, pl.ds(dst_row, tail_rows)], sem_tail.at[0])

        if kind == "v":
            pl.when(writes_tail)(lambda: tail_copy().start())

        def chunk(c, carry):
            for hh in range(heads_per_chunk):
                _finish_head(kind, dil, c * heads_per_chunk + hh, src, w, rope_ref, qkv_ref, stage.at[hh])
            col0 = pl.multiple_of(c * COL_CHUNK, COL_CHUNK)
            z = jnp.dot(xn_ref[...], w_ref[:, pl.ds(col0, COL_CHUNK)], preferred_element_type=F32)
            if to_res:
                for hh in range(heads_per_chunk):
                    res[slot, c * heads_per_chunk + hh] = z[:, hh * HEAD_DIM:(hh + 1) * HEAD_DIM]
            else:
                rest_ref[:, pl.ds(col0, COL_CHUNK)] = z
            return carry
        lax.fori_loop(0, TILE // COL_CHUNK, chunk, 0)

        @pl.when(writes_tail)
        def _tail():
            if kind == "qk":
                tail_copy().start()
            tail_copy().wait()

    pl.when(j == 0)(lambda: run(True, None))
    for g in range(N_GROUPS):
        pl.when(jnp.logical_or(j == 3 * g + 1, j == 3 * g + 2))(functools.partial(run, True, ("qk", g)))
        pl.when(j == 3 * g + 3)(functools.partial(run, 3 * g + 3 < N_QKV, ("v", g)))
    pl.when(j > N_QKV)(lambda: run(False, None))


def _proj_prompt(xn, w_bf16, rope, qw, kw, caches, news, batch, seq):
    m, d = xn.shape
    n = w_bf16.shape[1]
    assert m == batch * seq and seq % TILE == 0 and n % TILE == 0 and D_ATT == TILE
    tiles_per_seq = seq // TILE
    nj = n // TILE
    n_rest = nj - N_QKV
    grid = (m // TILE, nj)
    nc = len(caches)
    shift = news[0].shape[1]
    kinds, n_items = _roll_plan(caches, shift)
    assert n_items + ROLL_BUFFERS < grid[0] * grid[1] and all(nw.shape == news[0].shape for nw in news)
    assert all(w % TILE == 0 or TILE % w == 0 for w, _ in GROUPS)

    def w_col(i, j):
        return (0, jnp.where(j < N_QKV, j, jnp.where(j < nj - 1, j + 1, N_QKV)))

    any_spec = pl.BlockSpec(memory_space=pl.ANY)
    in_specs = [
        pl.BlockSpec((TILE, d), lambda i, j: (i, 0)),
        pl.BlockSpec((d, TILE), w_col),
        pl.BlockSpec((3, TILE, HEAD_DIM), lambda i, j: (0, i % tiles_per_seq, 0)),
        pl.BlockSpec((N_GROUPS, HEAD_DIM), lambda i, j: (0, 0)),
        pl.BlockSpec((N_GROUPS, HEAD_DIM), lambda i, j: (0, 0)),
        *([any_spec] * (2 * nc)),
    ]
    out_specs = [
        pl.BlockSpec((None, None, TILE, TILE), lambda i, j: (jnp.clip(j - 1, 0, N_QKV - 1), i, 0, 0)),
        pl.BlockSpec((TILE, TILE), lambda i, j: (i, jnp.clip(j - N_QKV, 0, n_rest - 1))),
        *([any_spec] * (N_GROUPS + nc)),
    ]
    out_shape = [
        jax.ShapeDtypeStruct((N_QKV, m // TILE, TILE, TILE), BF16),
        jax.ShapeDtypeStruct((m, n_rest * TILE), F32),
        *[jax.ShapeDtypeStruct((batch, 2, N_HEADS, w, HEAD_DIM), F32) for w, _ in GROUPS],
        *[jax.ShapeDtypeStruct(c.shape, c.dtype) for c in caches],
    ]
    row_shape = caches[0].shape[2:]
    max_rows = max(k[1] for k in kinds)
    scratch = [
        pltpu.VMEM((2, N_HEADS, TILE, HEAD_DIM), F32),
        pltpu.VMEM((COL_CHUNK // HEAD_DIM, TILE, HEAD_DIM), F32),
        pltpu.VMEM((ROLL_BUFFERS, max_rows) + row_shape, F32),
        pltpu.VMEM((nc,) + news[0].shape, F32),
        pltpu.SemaphoreType.DMA((ROLL_BUFFERS,)),
        pltpu.SemaphoreType.DMA((ROLL_BUFFERS,)),
        pltpu.SemaphoreType.DMA((2, nc)),
        pltpu.SemaphoreType.DMA((1,)),
    ]
    row_bytes = 4 * row_shape[0] * row_shape[1]
    est = (2 * TILE * d * 2 + 2 * d * TILE * 2 + 2 * TILE * 3 * HEAD_DIM * 4 + 2 * TILE * TILE * 2 + 2 * TILE * TILE * 4
           + 2 * TILE * TILE * 4 + (ROLL_BUFFERS * max_rows + nc * news[0].shape[0] * shift) * row_bytes + (6 << 20))
    out = pl.pallas_call(
        functools.partial(_proj_prompt_body, roll=(kinds, shift, nc), tiles_per_seq=tiles_per_seq, n_rest=n_rest),
        grid=grid,
        in_specs=in_specs,
        out_specs=out_specs,
        out_shape=out_shape,
        scratch_shapes=scratch,
        compiler_params=pltpu.CompilerParams(
            dimension_semantics=("arbitrary", "arbitrary"),
            vmem_limit_bytes=_vmem_limit(est)),
        name="proj_prompt",
    )(xn, w_bf16, rope, qw, kw, *caches, *news)
    return out[0], out[1], out[2:2 + N_GROUPS], out[2 + N_GROUPS:]


def _subsequence_rows(cur_ref, prev_ref, dil, res, l0, n):
    per = TILE // dil
    pieces = []
    l = l0
    while l < l0 + n:
        ll = l + TILES_PER_SUPER * per if l < 0 else l
        t, off = divmod(ll, per)
        stop = 0 if l < 0 else l0 + n
        take = min(per - off, stop - l)
        if l < 0:
            ref, t = prev_ref, t - (TILES_PER_SUPER - prev_ref.shape[0])
        else:
            ref = cur_ref
        pieces.append(ref[t, res * per + off:res * per + off + take, :])
        l += take
    return pieces[0] if len(pieces) == 1 else jnp.concatenate(pieces, axis=0)


def _attn_prompt_body(*refs):
    cur = refs[:N_QKV]
    prev = refs[N_QKV:N_QKV + 2 * N_GROUPS]
    o_ref, obuf, lbuf = refs[N_QKV + 2 * N_GROUPS:]
    sb = pl.program_id(2)
    qi = lax.broadcasted_iota(jnp.int32, (BAND, 2 * BAND), 0)
    kj = lax.broadcasted_iota(jnp.int32, (BAND, 2 * BAND), 1)
    hi = qi + BAND
    lo_first = jnp.maximum(qi, jnp.where(sb == 0, BAND, 0))
    nt = (((1,), (1,)), ((), ()))
    for g, (_, dil) in enumerate(GROUPS):
        q_ref, k_ref, v_ref = cur[3 * g:3 * g + 3]
        kp_ref, vp_ref = prev[2 * g:2 * g + 2]
        for res in range(dil):
            for blk in range(SUPER // (BAND * dil)):
                l0 = BAND * blk
                q = _subsequence_rows(q_ref, None, dil, res, l0, BAND)
                kc = _subsequence_rows(k_ref, kp_ref, dil, res, l0 - BAND, 2 * BAND)
                vc = _subsequence_rows(v_ref, vp_ref, dil, res, l0 - BAND, 2 * BAND)
                s = lax.dot_general(q, kc, nt, preferred_element_type=F32) * SCALE
                lo = lo_first if blk == 0 else qi
                s = jnp.where(kj >= lo, jnp.where(kj <= hi, s, NEG_INF), NEG_INF)
                m = jnp.max(s, axis=-1, keepdims=True)
                p = jnp.exp(s - m)
                l = jnp.sum(p, axis=-1, keepdims=True)
                o = jnp.dot((p / l).astype(BF16), vc, preferred_element_type=F32)
                start = dil * l0 + res
                rows = pl.ds(start, BAND, stride=dil) if dil > 1 else pl.ds(start, BAND)
                obuf[g, rows, :] = o
                lbuf[g, rows, :] = jnp.broadcast_to(m + jnp.log(l), (BAND, HEAD_DIM))

    rows_per_step = 256

    def combine(i, carry):
        rows = pl.ds(pl.multiple_of(i * rows_per_step, rows_per_step), rows_per_step)
        w0, w1, w2 = _group_weights(lbuf[0, rows, :], lbuf[1, rows, :], lbuf[2, rows, :])
        o_ref[rows, :] = (w0 * obuf[0, rows, :] + w1 * obuf[1, rows, :]) + w2 * obuf[2, rows, :]
        return carry
    lax.fori_loop(0, SUPER // rows_per_step, combine, 0)


def _attn_prompt(qkv, batch, seq):
    assert seq % SUPER == 0 and SUPER % TILE == 0
    nsb = seq // SUPER
    in_specs = [
        pl.BlockSpec((None, TILES_PER_SUPER, TILE, HEAD_DIM), lambda b, h, sb, jj=jj: (jj, b * nsb + sb, 0, h))
        for jj in range(N_QKV)
    ]
    args = [qkv] * N_QKV
    prev_bytes = 0
    for g, (_, dil) in enumerate(GROUPS):
        n_prev = -(-BAND * dil // TILE)
        assert TILES_PER_SUPER % n_prev == 0
        per_super = TILES_PER_SUPER // n_prev
        for which in (1, 2):
            in_specs.append(pl.BlockSpec(
                (None, n_prev, TILE, HEAD_DIM),
                lambda b, h, sb, jj=3 * g + which, per_super=per_super:
                    (jj, jnp.maximum((b * nsb + sb) * per_super - 1, b * nsb * per_super), 0, h)))
            args.append(qkv)
            prev_bytes += n_prev * TILE * HEAD_DIM * 2
    est = (2 * (N_QKV * TILES_PER_SUPER * TILE * HEAD_DIM * 2 + prev_bytes + SUPER * HEAD_DIM * 4)
           + 2 * N_GROUPS * SUPER * HEAD_DIM * 4 + (12 << 20))
    return pl.pallas_call(
        _attn_prompt_body,
        grid=(batch, N_HEADS, nsb),
        in_specs=in_specs,
        out_specs=pl.BlockSpec((SUPER, HEAD_DIM), lambda b, h, sb: (b * nsb + sb, h)),
        out_shape=jax.ShapeDtypeStruct((batch * seq, D_ATT), F32),
        scratch_shapes=[
            pltpu.VMEM((N_GROUPS, SUPER, HEAD_DIM), F32),
            pltpu.VMEM((N_GROUPS, SUPER, HEAD_DIM), F32),
        ],
        compiler_params=pltpu.CompilerParams(
            dimension_semantics=("arbitrary", "arbitrary", "arbitrary"),
            vmem_limit_bytes=_vmem_limit(est)),
        name="attn_prompt",
    )(*args)


def _attn_sample_body(qkv_ref, c0_ref, c1_ref, c2_ref, rope_ref, qw_ref, kw_ref,
                      o_ref, n0_ref, n1_ref, n2_ref):
    t_new = qkv_ref.shape[0]
    cache_refs = (c0_ref, c1_ref, c2_ref)
    new_refs = (n0_ref, n1_ref, n2_ref)
    outs = [[None] * N_GROUPS for _ in range(t_new)]
    lses = [[None] * N_GROUPS for _ in range(t_new)]
    for g, (_, dil) in enumerate(GROUPS):
        qs, ks, vs = [], [], []
        for t in range(t_new):
            c, sa, sbn = rope_ref[0, t:t + 1, :], rope_ref[1, t:t + 1, :], rope_ref[2, t:t + 1, :]
            qs.append(_norm_rope(qkv_ref[t, 3 * g], qw_ref[g:g + 1, :], c, sa, sbn))
            ks.append(_norm_rope(qkv_ref[t, 3 * g + 1], kw_ref[g:g + 1, :], c, sa, sbn))
            vs.append(qkv_ref[t, 3 * g + 2])
            new_refs[g][t, 0:N_HEADS, :] = ks[t]
            new_refs[g][t, N_HEADS:2 * N_HEADS, :] = vs[t]
        for t in range(t_new):
            res = t % dil
            first = -((res - t) // dil)
            kc = cache_refs[g][first:, res, 0:N_HEADS, :]
            vc = cache_refs[g][first:, res, N_HEADS:2 * N_HEADS, :]
            q = qs[t] * SCALE
            s_c = jnp.sum(kc * q[None], axis=-1, keepdims=True)
            new_keys = [tp for tp in range(t + 1) if (t - tp) % dil == 0]
            s_n = [jnp.sum(ks[tp] * q, axis=-1, keepdims=True) for tp in new_keys]
            m = jnp.max(s_c, axis=0)
            for s in s_n:
                m = jnp.maximum(m, s)
            p_c = jnp.exp(s_c - m[None])
            p_n = [jnp.exp(s - m) for s in s_n]
            l = jnp.sum(p_c, axis=0)
            for p in p_n:
                l = l + p
            o = jnp.sum(p_c * vc, axis=0)
            for p, tp in zip(p_n, new_keys):
                o = o + p * vs[tp]
            outs[t][g] = o / l
            lses[t][g] = m + jnp.log(l)
    for t in range(t_new):
        w0, w1, w2 = _group_weights(*lses[t])
        o_ref[t] = (w0 * outs[t][0] + w1 * outs[t][1]) + w2 * outs[t][2]


def _attn_sample(qkv, caches, rope, qw, kw):
    db, t_new = qkv.shape[:2]
    views, cache_specs, cache_bytes = [], [], 0
    for (window, dil), cache in zip(GROUPS, caches):
        assert cache.shape[1] == window == BAND * dil and t_new <= BAND
        n_res = min(dil, t_new)
        views.append(cache.reshape(db, BAND, dil, 2 * N_HEADS, HEAD_DIM))
        cache_specs.append(pl.BlockSpec((None, BAND, n_res, 2 * N_HEADS, HEAD_DIM), lambda b: (b, 0, 0, 0, 0)))
        cache_bytes += BAND * n_res * 2 * N_HEADS * HEAD_DIM * 4
    new_spec = pl.BlockSpec((None, t_new, 2 * N_HEADS, HEAD_DIM), lambda b: (b, 0, 0, 0))
    new_shape = jax.ShapeDtypeStruct((db, t_new, 2 * N_HEADS, HEAD_DIM), F32)
    return pl.pallas_call(
        _attn_sample_body,
        grid=(db,),
        in_specs=[
            pl.BlockSpec((None, t_new, 3 * N_GROUPS, N_HEADS, HEAD_DIM), lambda b: (b, 0, 0, 0, 0)),
            *cache_specs,
            pl.BlockSpec((3, t_new, HEAD_DIM), lambda b: (0, 0, 0)),
            pl.BlockSpec((N_GROUPS, HEAD_DIM), lambda b: (0, 0)),
            pl.BlockSpec((N_GROUPS, HEAD_DIM), lambda b: (0, 0)),
        ],
        out_specs=[pl.BlockSpec((None, t_new, N_HEADS, HEAD_DIM), lambda b: (b, 0, 0, 0)),
                   new_spec, new_spec, new_spec],
        out_shape=[jax.ShapeDtypeStruct((db, t_new, N_HEADS, HEAD_DIM), F32),
                   new_shape, new_shape, new_shape],
        compiler_params=pltpu.CompilerParams(
            dimension_semantics=("arbitrary",), vmem_limit_bytes=_vmem_limit(2 * cache_bytes + (16 << 20))),
        name="attn_sample",
    )(qkv, *views, rope, qw, kw)


def _sigmoid(x):
    return 1.0 / (1.0 + jnp.exp(-x))


def _merge_body(*refs, sample, rows_per_seq):
    if sample:
        (oa_ref, ag_ref, hc_ref, bc_ref, cc_ref, cg_ref, mla_ref, mlc_ref, x_ref, cw_ref,
         wa_ref, wc_ref, wo_ref, p1_ref, p2_ref, y_ref, u_ref, hist_ref) = refs
    else:
        (oa_ref, ag_ref, hc_ref, bc_ref, cc_ref, cg_ref, mla_ref, mlc_ref, x_ref, cw_ref,
         wa_ref, wc_ref, wo_ref, y_ref, u_ref, hist_ref) = refs
    tm = x_ref.shape[0]

    ag = ag_ref[...]
    ain = (oa_ref[...] * (ag * _sigmoid(ag))).astype(BF16)
    a = jnp.dot(ain, wa_ref[...], preferred_element_type=F32)

    u = cc_ref[...] * hc_ref[...]
    if sample:
        hist_ref[...] = jnp.zeros(hist_ref.shape, F32)
    else:
        @pl.when(pl.program_id(0) % (rows_per_seq // tm) == 0)
        def _sequence_start():
            hist_ref[...] = jnp.zeros(hist_ref.shape, F32)
    ext = jnp.concatenate([hist_ref[...], u], axis=0)
    u1 = pltpu.roll(ext, 1, 0)[SUBLANES:, :]
    u2 = pltpu.roll(ext, 2, 0)[SUBLANES:, :]
    if sample:
        t = lax.broadcasted_iota(jnp.int32, (tm, 1), 0) % rows_per_seq
        u1 = jnp.where(t < 1, p1_ref[...], u1)
        u2 = jnp.where(t < 2, p2_ref[...], u2)
        u_ref[...] = u
    else:
        tail = u[tm - SUBLANES:, :]
        hist_ref[...] = tail
        u_ref[...] = tail
    conv = (cw_ref[0:1, :] * u2 + cw_ref[1:2, :] * u1) + cw_ref[2:3, :] * u
    cg = cg_ref[...]
    cin = ((bc_ref[...] * conv) * (cg * _sigmoid(cg))).astype(BF16)
    cbr = jnp.dot(cin, wc_ref[...], preferred_element_type=F32)

    m = _sigmoid(mla_ref[...]) * a + _sigmoid(mlc_ref[...]) * cbr
    y_ref[...] = x_ref[...] + jnp.dot(m.astype(BF16), wo_ref[...], preferred_element_type=F32)


def _merge(x, z, o_att, conv_w, wa, wc, wo, *, tm, rows_per_seq, gate_col, conv_col, state=None):
    m_rows, d = x.shape
    sample = state is not None
    assert m_rows % tm == 0 and (sample or rows_per_seq % tm == 0)
    row = lambda i: (i, 0)
    const = lambda i: (0, 0)
    zcol = lambda j: (lambda i: (i, j))
    resident = dict(pipeline_mode=pl.Buffered(1))
    in_specs = [
        pl.BlockSpec((tm, D_ATT), row),
        pl.BlockSpec((tm, D_ATT), zcol(gate_col)),
        pl.BlockSpec((tm, d), zcol(conv_col)),
        pl.BlockSpec((tm, d), zcol(conv_col + 1)),
        pl.BlockSpec((tm, d), zcol(conv_col + 2)),
        pl.BlockSpec((tm, d), zcol(conv_col + 3)),
        pl.BlockSpec((tm, d), zcol(conv_col + 4)),
        pl.BlockSpec((tm, d), zcol(conv_col + 5)),
        pl.BlockSpec((tm, d), row),
        pl.BlockSpec((CONV_WIDTH, d), const),
        pl.BlockSpec((D_ATT, d), const, **resident),
        pl.BlockSpec((d, d), const, **resident),
        pl.BlockSpec((d, d), const, **resident),
    ]
    args = [o_att, z, z, z, z, z, z, z, x, conv_w, wa, wc, wo]
    if sample:
        in_specs += [pl.BlockSpec((tm, d), row), pl.BlockSpec((tm, d), row)]
        args += list(state)
        u_spec = pl.BlockSpec((tm, d), row)
        u_shape = jax.ShapeDtypeStruct((m_rows, d), F32)
    else:
        per_seq = rows_per_seq // tm
        u_spec = pl.BlockSpec((None, SUBLANES, d), lambda i: (i // per_seq, 0, 0))
        u_shape = jax.ShapeDtypeStruct((m_rows // rows_per_seq, SUBLANES, d), F32)
    est = (2 * tm * (2 * D_ATT + 8 * d) * 4 + (D_ATT * d + 2 * d * d) * 2
           + 10 * tm * d * 4 + (4 << 20))
    return pl.pallas_call(
        functools.partial(_merge_body, sample=sample, rows_per_seq=rows_per_seq),
        grid=(m_rows // tm,),
        in_specs=in_specs,
        out_specs=[pl.BlockSpec((tm, d), row), u_spec],
        out_shape=[jax.ShapeDtypeStruct((m_rows, d), F32), u_shape],
        scratch_shapes=[pltpu.VMEM((SUBLANES, d), F32)],
        compiler_params=pltpu.CompilerParams(
            dimension_semantics=("arbitrary",), vmem_limit_bytes=_vmem_limit(est)),
        name="merge_sample" if sample else "merge_prompt",
    )(*args)


def _layer(xp, xs, caches, state, norm_w, w_in, q_norm_w, k_norm_w, conv_w, w_att_proj, w_conv_proj, w_out):
    b, s, d = xp.shape
    db, t_new, _ = xs.shape
    n = w_in.shape[1]
    assert n == N_QKV * D_ATT + D_ATT + 6 * d and d % D_ATT == 0
    wa, wc, wo = w_att_proj.astype(BF16), w_conv_proj.astype(BF16), w_out.astype(BF16)
    g = norm_w.reshape(1, d)
    xp2, xs2 = xp.reshape(b * s, d), xs.reshape(db * t_new, d)

    rows_s = db * t_new
    zs, w_in_b = _proj_plain(_rmsnorm(xs2, g, rows_s), w_in, TILE)
    rope_s = _rope_tables(PAST_LEN + jnp.arange(t_new, dtype=F32))
    qkv_s = zs[:, :N_QKV * D_ATT].reshape(db, t_new, N_QKV, N_HEADS, HEAD_DIM)
    flat_caches = [c.reshape(db, c.shape[1], 2 * N_HEADS, HEAD_DIM) for c in caches]
    o_att_s, *kv_new = _attn_sample(qkv_s, flat_caches, rope_s, q_norm_w, k_norm_w)
    p1 = jnp.pad(state[:, 1:2], ((0, 0), (0, t_new - 1), (0, 0))).reshape(rows_s, d)
    p2 = jnp.pad(state, ((0, 0), (0, t_new - 2), (0, 0))).reshape(rows_s, d)
    ys, u_s = _merge(xs2, zs, o_att_s.reshape(rows_s, D_ATT), conv_w, wa, wc, wo, tm=rows_s, rows_per_seq=t_new,
                     gate_col=N_QKV, conv_col=(N_QKV * D_ATT + D_ATT) // d, state=(p1, p2))
    conv_s = u_s.reshape(db, t_new, d)[:, t_new - (CONV_WIDTH - 1):]

    rope_p = _rope_tables(jnp.arange(s, dtype=F32))
    qkv_p, rest_p, tails, rolled = _proj_prompt(_rmsnorm(xp2, g, 512), w_in_b, rope_p, q_norm_w, k_norm_w,
                                                flat_caches, kv_new, b, s)
    kv_s = [c.reshape(db, -1, 2, N_HEADS, HEAD_DIM) for c in rolled]
    kv_p = [jnp.transpose(t, (0, 3, 1, 2, 4)) for t in tails]
    o_att_p = _attn_prompt(qkv_p, b, s)
    yp, u_tail = _merge(xp2, rest_p, o_att_p, conv_w, wa, wc, wo, tm=128, rows_per_seq=s,
                        gate_col=6 * d // D_ATT, conv_col=0)
    conv_p = u_tail[:, SUBLANES - (CONV_WIDTH - 1):]

    return yp.reshape(b, s, d), ys.reshape(db, t_new, d), kv_p, conv_p, kv_s, conv_s


def kernel(x_prompt, x_sample, cache_kv_w128, cache_kv_w512, cache_kv_w2048, state_conv, norm_w, w_in, q_norm_w,
           k_norm_w, conv_w, w_att_proj, w_conv_proj, w_out):
    depth = norm_w.shape[0]
    caches = (cache_kv_w128, cache_kv_w512, cache_kv_w2048)
    yp, ys = x_prompt, x_sample
    kv_p = [[] for _ in GROUPS]
    kv_s = [[] for _ in GROUPS]
    conv_p, conv_s = [], []
    for l in range(depth):
        yp, ys, kvp_l, cp_l, kvs_l, cs_l = _layer(
            yp, ys, [c[l] for c in caches], state_conv[l], norm_w[l], w_in[l], q_norm_w[l], k_norm_w[l],
            conv_w[l], w_att_proj[l], w_conv_proj[l], w_out[l])
        for gi in range(N_GROUPS):
            kv_p[gi].append(kvp_l[gi])
            kv_s[gi].append(kvs_l[gi])
        conv_p.append(cp_l)
        conv_s.append(cs_l)
    stack = lambda xs: xs[0][None] if len(xs) == 1 else jnp.stack(xs, axis=0)
    return (yp, ys, stack(kv_p[0]), stack(kv_p[1]), stack(kv_p[2]), stack(conv_p),
            stack(kv_s[0]), stack(kv_s[1]), stack(kv_s[2]), stack(conv_s))
```
